```python
import math
import jax
import jax.numpy as jnp
from jax import lax
import numpy as np

D_MODEL = 4096
BATCH = 32
SEQ = 256
DEPTH = 4
DEC_BATCH = 8
DEC_SEQ = 4096
PAST_LEN = 256

GRID_W = 64
A_WIDTH = D_MODEL // 2
A_HEAD_DIM = 128
A_HEADS = A_WIDTH // A_HEAD_DIM
HGRN_CHUNK = 64
B_WIDTH = D_MODEL // 2
SGU_CHUNK = 128
B_GROUPS = 16
B_GROUP_DIM = B_WIDTH // B_GROUPS
C_HEADS = D_MODEL // 128
Q_LORA = D_MODEL // 4
KV_LORA = 512
NOPE_DIM = 128
ROPE_DIM = 64
V_DIM = 128
ROPE_THETA = 10000.0
Q_BLOCK = 128
N_EXPERTS = 48
N_GROUPS = 8
EXPERTS_PER_GROUP = N_EXPERTS // N_GROUPS
TOP_K = 2
GROUP_SCORE_TOPK = 2
EXPERT_FF = D_MODEL // 4
MOE_BLOCK = 128
N_EVEN = (DEPTH + 1) // 2
N_ODD = DEPTH // 2
ADA_MULT = 6
DEEPNORM_ALPHA = (2.0 * DEPTH) ** 0.25
DEEPNORM_BETA = (8.0 * DEPTH) ** -0.25
LN_EPS = 1e-6
F32 = jnp.float32

kernel_name = "hybrid_hgrn2_sgu_mla_moe_diffusion_step"


def _layernorm(x, g, b):
    xf = x.astype(F32)
    mu = jnp.mean(xf, axis=-1, keepdims=True)
    var = jnp.mean(jnp.square(xf - mu), axis=-1, keepdims=True)
    return ((xf - mu) * lax.rsqrt(var + LN_EPS) * g.astype(F32) + b.astype(F32)).astype(x.dtype)


def _rmsnorm(x, g):
    xf = x.astype(F32)
    return (xf * lax.rsqrt(jnp.mean(xf * xf, axis=-1, keepdims=True) + LN_EPS) * g.astype(F32)).astype(x.dtype)


def _modulation(cond, w, b):
    m = jax.nn.silu(cond.astype(F32)).astype(w.dtype) @ w + b
    return jnp.split(m[:, None, :], ADA_MULT, axis=-1)


def _axial_rope(seq_len):
    rows = seq_len // GRID_W
    t = jnp.arange(rows * GRID_W)
    row = (t // GRID_W).astype(F32)
    col = (t % GRID_W).astype(F32)
    n_freq = ROPE_DIM // 4
    inv = ROPE_THETA ** (-jnp.arange(n_freq, dtype=F32) / n_freq)
    ang = jnp.stack([row[:, None] * inv, col[:, None] * inv], axis=1)
    return jnp.cos(ang), jnp.sin(ang)


def _apply_rope(x, cos, sin):
    xf = x.astype(F32).reshape(x.shape[:-1] + (2, 2, ROPE_DIM // 4))
    x1, x2 = xf[..., 0, :], xf[..., 1, :]
    out = jnp.stack([x1 * cos - x2 * sin, x2 * cos + x1 * sin], axis=-2)
    return out.reshape(x.shape).astype(x.dtype)


def _lower_bounds(lb_raw):
    p = jax.nn.softmax(lb_raw.astype(F32), axis=0)
    return jnp.cumsum(p, axis=0) - p[0:1]


def _hgrn_scan(q, k, v, logf, s0):
    B, S, H, DK = q.shape
    DV = v.shape[-1]
    n = S // HGRN_CHUNK

    def to_chunks(a):
        return a.astype(F32).reshape(B, n, HGRN_CHUNK, H, a.shape[-1]).transpose(1, 0, 3, 2, 4)

    qc, kc, vc, gc = to_chunks(q), to_chunks(k), to_chunks(v), to_chunks(logf)
    mask = jnp.tril(jnp.ones((HGRN_CHUNK, HGRN_CHUNK), dtype=bool))[:, :, None]

    def step(state, inp):
        qb, kb, vb, gb = inp
        b = jnp.cumsum(gb, axis=2)
        diff = b[:, :, :, None, :] - b[:, :, None, :, :]
        decay = jnp.exp(jnp.where(mask, diff, -jnp.inf))
        scores = jnp.einsum('bhtd,bhsd,bhtsd->bhts', qb, kb, decay)
        o = jnp.einsum('bhts,bhsv->bhtv', scores, vb) + jnp.einsum('bhtd,bhdv->bhtv', qb * jnp.exp(b), state)
        b_last = b[:, :, -1:, :]
        state = state * jnp.exp(b_last[:, :, 0, :])[..., None] + jnp.einsum('bhsd,bhsv->bhdv', kb * jnp.exp(b_last - b), vb)
        return state, o

    s_final, oc = lax.scan(step, s0.astype(F32), (qc, kc, vc, gc))
    return oc.transpose(1, 0, 3, 2, 4).reshape(B, S, H, DV), s_final


def _chunk_sgu(u, v, ln_g, ln_b, w_s, b_s):
    u = jax.nn.gelu(u)
    v = _layernorm(jax.nn.gelu(v), ln_g, ln_b)
    B, S, _ = v.shape
    vc = v.reshape(B, S // SGU_CHUNK, SGU_CHUNK, B_GROUPS, B_GROUP_DIM)
    mixed = jnp.einsum('gts,bnsgc->bntgc', w_s, vc) + b_s.T[:, :, None]
    return u * mixed.reshape(B, S, B_WIDTH)


def _even_mixer(h, w_in, lb_fwd, lb_bwd, norm_g, sgu_g, sgu_bn, sgu_w, sgu_b, w_out, s0_fwd, s0_bwd):
    B, S, _ = h.shape
    splits = [A_WIDTH, 2 * A_WIDTH, 3 * A_WIDTH, 4 * A_WIDTH, 5 * A_WIDTH, 5 * A_WIDTH + B_WIDTH]
    q_r, i_r, zf, zb, g_r, u, v = jnp.split(h @ w_in, splits, axis=-1)

    def heads(a):
        return a.astype(F32).reshape(B, S, A_HEADS, A_HEAD_DIM)

    def gates(z, lb):
        z = heads(z)
        lb = lb.reshape(A_HEADS, A_HEAD_DIM)
        logf = jnp.logaddexp(jnp.log(lb), jnp.log1p(-lb) + jax.nn.log_sigmoid(z))
        k = (1.0 - lb) * jax.nn.sigmoid(-z)
        return k, logf

    q = jax.nn.silu(heads(q_r))
    iv = heads(i_r)
    kf, gf = gates(zf, lb_fwd)
    kb, gb = gates(zb, lb_bwd)
    o_f, s_f = _hgrn_scan(q, kf, iv, gf, s0_fwd)
    o_b, s_b = _hgrn_scan(q[:, ::-1], kb[:, ::-1], iv[:, ::-1], gb[:, ::-1], s0_bwd)
    o = _rmsnorm(o_f + o_b[:, ::-1], norm_g.reshape(A_HEADS, A_HEAD_DIM))
    o_a = (o.reshape(B, S, A_WIDTH) * jax.nn.silu(g_r.astype(F32))).astype(h.dtype)
    o_s = _chunk_sgu(u, v, sgu_g, sgu_bn, sgu_w, sgu_b)
    return jnp.concatenate([o_a, o_s], axis=-1) @ w_out, s_f, s_b


def _blocked_attention(q, k, v):
    B, Sq, H, Dk = q.shape
    nb = Sq // Q_BLOCK
    qb = q.reshape(B, nb, Q_BLOCK, H, Dk).swapaxes(0, 1)
    scale = 1.0 / math.sqrt(Dk)

    def one(qblk):
        s = jnp.einsum('bqhd,bkhd->bhqk', qblk, k).astype(F32) * scale
        p = jax.nn.softmax(s, axis=-1).astype(v.dtype)
        return jnp.einsum('bhqk,bkhd->bqhd', p, v)

    o = lax.map(one, qb)
    return o.swapaxes(0, 1).reshape(B, Sq, H, v.shape[-1])


def _mla_project(h, w_in, qn_g, kvn_g, w_uq):
    B, S, _ = h.shape
    cq, ckv, kpe = jnp.split(h @ w_in, [Q_LORA, Q_LORA + KV_LORA], axis=-1)
    q = (_rmsnorm(cq, qn_g) @ w_uq).reshape(B, S, C_HEADS, NOPE_DIM + ROPE_DIM)
    return q, _rmsnorm(ckv, kvn_g), kpe


def _mla_expand(ckv, w_ukv):
    B, S, _ = ckv.shape
    kv = (ckv @ w_ukv).reshape(B, S, C_HEADS, NOPE_DIM + V_DIM)
    return kv[..., :NOPE_DIM], kv[..., NOPE_DIM:]


def _mla_attend(q, k_nope, k_pe, v, w_o):
    B, S = q.shape[:2]
    k_pe_h = jnp.broadcast_to(k_pe[:, :, None, :], k_pe.shape[:2] + (C_HEADS, ROPE_DIM))
    o = _blocked_attention(q, jnp.concatenate([k_nope, k_pe_h], axis=-1), v)
    return o.reshape(B, S, C_HEADS * V_DIM) @ w_o


def _moe(h, w_router, b_router, w_gate, w_up, w_down):
    B, S, D = h.shape
    n_tok = B * S
    xt = h.reshape(n_tok, D)
    scores = jax.nn.sigmoid((xt @ w_router).astype(F32))
    sel = (scores + b_router.astype(F32)).reshape(n_tok, N_GROUPS, EXPERTS_PER_GROUP)
    group_score = lax.top_k(sel, GROUP_SCORE_TOPK)[0].sum(-1)
    g_idx = jnp.argmax(group_score, axis=-1)
    tok = jnp.arange(n_tok)
    _, local = lax.top_k(sel[tok, g_idx], TOP_K)
    expert_idx = g_idx[:, None] * EXPERTS_PER_GROUP + local
    gate = scores[tok[:, None], expert_idx]
    gate = gate / jnp.sum(gate, axis=-1, keepdims=True)
    flat_e = expert_idx.reshape(-1)
    flat_t = jnp.repeat(tok, TOP_K)
    order = jnp.argsort(flat_e)
    se, st, sg = flat_e[order], flat_t[order], gate.reshape(-1)[order]
    counts = jnp.bincount(flat_e, length=N_EXPERTS)
    padded = (counts + MOE_BLOCK - 1) // MOE_BLOCK * MOE_BLOCK
    pad_end = jnp.cumsum(padded)
    pad_start = pad_end - padded
    start = jnp.cumsum(counts) - counts
    dest = pad_start[se] + jnp.arange(n_tok * TOP_K) - start[se]
    n_rows = ((n_tok * TOP_K + N_EXPERTS * (MOE_BLOCK - 1) + MOE_BLOCK - 1) // MOE_BLOCK) * MOE_BLOCK
    n_blk = n_rows // MOE_BLOCK
    xp = jnp.zeros((n_rows, D), h.dtype).at[dest].set(xt[st])
    blk_expert = jnp.minimum(jnp.searchsorted(pad_end, jnp.arange(n_blk) * MOE_BLOCK, side='right'), N_EXPERTS - 1)

    def expert_block(args):
        xb, e = args
        return (jax.nn.silu(xb @ w_gate[e]) * (xb @ w_up[e])) @ w_down[e]

    yp = lax.map(expert_block, (xp.reshape(n_blk, MOE_BLOCK, D), blk_expert)).reshape(n_rows, D)
    y = jax.ops.segment_sum(yp[dest].astype(F32) * sg[:, None], st, num_segments=n_tok)
    return y.astype(h.dtype).reshape(B, S, D)


def setup_inputs(seed: int = 0) -> dict:
    key = jax.random.key(seed)
    ks = iter(jax.random.split(key, 40))

    def nrm(shape, scale=1.0):
        return jax.random.normal(next(ks), shape, F32) * scale

    ev_cols = 5 * A_WIDTH + 2 * B_WIDTH
    c_cols = Q_LORA + KV_LORA + ROPE_DIM
    return {
        "x_prompt": nrm((BATCH, SEQ, D_MODEL)),
        "x_sample": nrm((DEC_BATCH, DEC_SEQ, D_MODEL)),
        "state_hgrn": nrm((DEC_BATCH, N_EVEN, 2, A_HEADS, A_HEAD_DIM, A_HEAD_DIM), 0.5),
        "cache_ckv": nrm((DEC_BATCH, N_ODD, PAST_LEN, KV_LORA)),
        "cache_kpe": nrm((DEC_BATCH, N_ODD, PAST_LEN, ROPE_DIM)),
        "c": nrm((DEC_BATCH, D_MODEL)),
        "c_ctx": nrm((D_MODEL,)),
        "w_ada": nrm((DEPTH, D_MODEL, ADA_MULT * D_MODEL), 0.5 * D_MODEL ** -0.5),
        "b_ada": nrm((DEPTH, ADA_MULT * D_MODEL), 0.01),
        "ln_g": 1.0 + nrm((DEPTH, 2, D_MODEL), 0.02),
        "ln_b": nrm((DEPTH, 2, D_MODEL), 0.02),
        "w_in_ab": nrm((N_EVEN, D_MODEL, ev_cols), D_MODEL ** -0.5),
        "hgrn_lb": nrm((N_EVEN, 2, A_WIDTH)),
        "hgrn_norm_g": 1.0 + nrm((N_EVEN, A_WIDTH), 0.02),
        "sgu_ln_g": 1.0 + nrm((N_EVEN, B_WIDTH), 0.02),
        "sgu_ln_b": nrm((N_EVEN, B_WIDTH), 0.02),
        "sgu_w": nrm((N_EVEN, B_GROUPS, SGU_CHUNK, SGU_CHUNK), SGU_CHUNK ** -0.5),
        "sgu_b": 1.0 + nrm((N_EVEN, B_GROUPS, SGU_CHUNK), 0.02),
        "w_out_ab": nrm((N_EVEN, A_WIDTH + B_WIDTH, D_MODEL), (A_WIDTH + B_WIDTH) ** -0.5 * DEEPNORM_BETA),
        "w_in_c": nrm((N_ODD, D_MODEL, c_cols), D_MODEL ** -0.5),
        "q_norm_g": 1.0 + nrm((N_ODD, Q_LORA), 0.02),
        "kv_norm_g": 1.0 + nrm((N_ODD, KV_LORA), 0.02),
        "w_uq": nrm((N_ODD, Q_LORA, C_HEADS * (NOPE_DIM + ROPE_DIM)), Q_LORA ** -0.5),
        "w_ukv": nrm((N_ODD, KV_LORA, C_HEADS * (NOPE_DIM + V_DIM)), KV_LORA ** -0.5),
        "w_o_c": nrm((N_ODD, C_HEADS * V_DIM, D_MODEL), (C_HEADS * V_DIM) ** -0.5 * DEEPNORM_BETA),
        "w_router": nrm((D_MODEL, N_EXPERTS), D_MODEL ** -0.5),
        "b_router": nrm((N_EXPERTS,), 0.01),
        "w_gate_e": nrm((DEPTH, N_EXPERTS, D_MODEL, EXPERT_FF), D_MODEL ** -0.5),
        "w_up_e": nrm((DEPTH, N_EXPERTS, D_MODEL, EXPERT_FF), D_MODEL ** -0.5),
        "w_down_e": nrm((DEPTH, N_EXPERTS, EXPERT_FF, D_MODEL), EXPERT_FF ** -0.5 * DEEPNORM_BETA),
    }


def reference(x_prompt, x_sample, state_hgrn, cache_ckv, cache_kpe, c, c_ctx, w_ada, b_ada, ln_g, ln_b,
              w_in_ab, hgrn_lb, hgrn_norm_g, sgu_ln_g, sgu_ln_b, sgu_w, sgu_b, w_out_ab,
              w_in_c, q_norm_g, kv_norm_g, w_uq, w_ukv, w_o_c,
              w_router, b_router, w_gate_e, w_up_e, w_down_e):
    hp, hs = x_prompt, x_sample
    cos, sin = _axial_rope(x_sample.shape[1])
    zero_state = jnp.zeros((x_prompt.shape[0], A_HEADS, A_HEAD_DIM, A_HEAD_DIM), F32)
    lbs = _lower_bounds(hgrn_lb)
    hg_states, ckv_list, kpe_list = [], [], []
    for layer in range(DEPTH):
        j = layer // 2
        sh1p, sc1p, g1p, sh2p, sc2p, g2p = _modulation(c_ctx[None, :], w_ada[layer], b_ada[layer])
        sh1s, sc1s, g1s, sh2s, sc2s, g2s = _modulation(c, w_ada[layer], b_ada[layer])
        ap = hp * (1.0 + sc1p) + sh1p
        asm = hs * (1.0 + sc1s) + sh1s
        if layer % 2 == 0:
            ew = (w_in_ab[j], lbs[j, 0], lbs[j, 1], hgrn_norm_g[j], sgu_ln_g[j], sgu_ln_b[j], sgu_w[j], sgu_b[j], w_out_ab[j])
            mp, sfp, sbp = _even_mixer(ap, *ew, zero_state, zero_state)
            ms, _, _ = _even_mixer(asm, *ew, state_hgrn[:, j, 0], state_hgrn[:, j, 1])
            hg_states.append(jnp.stack([sfp, sbp], axis=1).astype(x_prompt.dtype))
        else:
            qp, ckvp, kpep = _mla_project(ap, w_in_c[j], q_norm_g[j], kv_norm_g[j], w_uq[j])
            knp, vp = _mla_expand(ckvp, w_ukv[j])
            mp = _mla_attend(qp, knp, kpep, vp, w_o_c[j])
            ckv_list.append(ckvp)
            kpe_list.append(kpep)
            qs, ckvs, kpes = _mla_project(asm, w_in_c[j], q_norm_g[j], kv_norm_g[j], w_uq[j])
            qs = jnp.concatenate([qs[..., :NOPE_DIM], _apply_rope(qs[..., NOPE_DIM:], cos[:, None], sin[:, None])], axis=-1)
            kpes = _apply_rope(kpes, cos, sin)
            kns, vs = _mla_expand(ckvs, w_ukv[j])
            knc, vcx = _mla_expand(cache_ckv[:, j], w_ukv[j])
            ms = _mla_attend(qs, jnp.concatenate([kns, knc], axis=1), jnp.concatenate([kpes, cache_kpe[:, j]], axis=1),
                             jnp.concatenate([vs, vcx], axis=1), w_o_c[j])
        hp = _layernorm(DEEPNORM_ALPHA * hp + g1p * mp, ln_g[layer, 0], ln_b[layer, 0])
        hs = _layernorm(DEEPNORM_ALPHA * hs + g1s * ms, ln_g[layer, 0], ln_b[layer, 0])
        bp = hp * (1.0 + sc2p) + sh2p
        bs = hs * (1.0 + sc2s) + sh2s
        yp = _moe(bp, w_router, b_router, w_gate_e[layer], w_up_e[layer], w_down_e[layer])
        ys = _moe(bs, w_router, b_router, w_gate_e[layer], w_up_e[layer], w_down_e[layer])
        hp = _layernorm(DEEPNORM_ALPHA * hp + g2p * yp, ln_g[layer, 1], ln_b[layer, 1])
        hs = _layernorm(DEEPNORM_ALPHA * hs + g2s * ys, ln_g[layer, 1], ln_b[layer, 1])
    new_state_hgrn = jnp.stack(hg_states, axis=1)
    new_cache_ckv = jnp.stack(ckv_list, axis=1)
    new_cache_kpe = jnp.stack(kpe_list, axis=1)
    return (hp, hs, new_state_hgrn, new_cache_ckv, new_cache_kpe)
```

```python
import functools
import math

import jax
import jax.numpy as jnp
from jax import lax
from jax.experimental import pallas as pl
from jax.experimental.pallas import tpu as pltpu

F32 = jnp.float32
BF16 = jnp.bfloat16

D_MODEL = 4096
DEPTH = 4
GRID_W = 64
A_WIDTH = D_MODEL // 2
A_HEAD_DIM = 128
A_HEADS = A_WIDTH // A_HEAD_DIM
B_WIDTH = D_MODEL // 2
SGU_CHUNK = 128
B_GROUPS = 16
C_HEADS = D_MODEL // 128
Q_LORA = D_MODEL // 4
KV_LORA = 512
NOPE_DIM = 128
ROPE_DIM = 64
V_DIM = 128
ROPE_THETA = 10000.0
N_EXPERTS = 48
N_GROUPS = 8
EXPERTS_PER_GROUP = N_EXPERTS // N_GROUPS
TOP_K = 2
GROUP_SCORE_TOPK = 2
EXPERT_FF = D_MODEL // 4
ADA_MULT = 6
DEEPNORM_ALPHA = (2.0 * DEPTH) ** 0.25
LN_EPS = 1e-6

COND_ROWS = 4096
HGRN_BLOCK = 16
SUBLANES = 8
MOE_TM = 256
MOE_TF = 256
MOE_TN = 1024
VMEM_LIMIT = 56 * 1024 * 1024


def _cparams(n_axes):
    return pltpu.CompilerParams(dimension_semantics=("arbitrary",) * n_axes, vmem_limit_bytes=VMEM_LIMIT)


def _mm_kernel(x_ref, w_ref, o_ref):
    o_ref[...] = jnp.dot(x_ref[...], w_ref[...], preferred_element_type=F32).astype(o_ref.dtype)


def _mm(x, w, out_dtype, tm=512, tn=1024):
    M, K = x.shape
    N = w.shape[1]
    tm, tn = min(tm, M), min(tn, N)
    assert M % tm == 0 and N % tn == 0
    return pl.pallas_call(
        _mm_kernel,
        grid=(N // tn, M // tm),
        in_specs=[pl.BlockSpec((tm, K), lambda n, m: (m, 0)),
                  pl.BlockSpec((K, tn), lambda n, m: (0, n))],
        out_specs=pl.BlockSpec((tm, tn), lambda n, m: (m, n)),
        out_shape=jax.ShapeDtypeStruct((M, N), out_dtype),
        compiler_params=_cparams(2),
        name="dense_mm",
    )(x, w)


def _ada_kernel(x_ref, w_ref, b_ref, o_ref):
    o_ref[...] = jnp.dot(x_ref[...], w_ref[...].astype(BF16), preferred_element_type=F32) + b_ref[...]


def _ada_mm(x, w, b, tn=1024):
    R, K = x.shape
    N = w.shape[1]
    return pl.pallas_call(
        _ada_kernel,
        grid=(N // tn,),
        in_specs=[pl.BlockSpec((R, K), lambda n: (0, 0)),
                  pl.BlockSpec((K, tn), lambda n: (0, n)),
                  pl.BlockSpec((1, tn), lambda n: (0, n))],
        out_specs=pl.BlockSpec((R, tn), lambda n: (0, n)),
        out_shape=jax.ShapeDtypeStruct((R, N), F32),
        compiler_params=_cparams(1),
        name="ada_mm",
    )(x, w, b)


def _split3(a):
    hi = a.astype(BF16)
    r1 = a - hi.astype(F32)
    mid = r1.astype(BF16)
    lo = (r1 - mid.astype(F32)).astype(BF16)
    return hi, mid, lo


def _router_kernel(x_ref, w_ref, o_ref):
    xh, xm, _ = _split3(x_ref[...])
    wh, wm, _ = _split3(w_ref[...])
    dot = functools.partial(jnp.dot, preferred_element_type=F32)
    o_ref[...] = dot(xh, wh) + (dot(xh, wm) + dot(xm, wh))


def _router_mm(x, w, tm=256):
    M, K = x.shape
    N = w.shape[1]
    return pl.pallas_call(
        _router_kernel,
        grid=(M // tm,),
        in_specs=[pl.BlockSpec((tm, K), lambda m: (m, 0)),
                  pl.BlockSpec((K, N), lambda m: (0, 0))],
        out_specs=pl.BlockSpec((tm, N), lambda m: (m, 0)),
        out_shape=jax.ShapeDtypeStruct((M, N), F32),
        compiler_params=_cparams(1),
        name="router_mm",
    )(x, w)


def _sigmoid(x):
    return 1.0 / (1.0 + jnp.exp(-x))


def _hgrn_kernel(*refs, seq, has_state, emit_state):
    q_ref, i_ref, zf_ref, zb_ref, g_ref, tab_ref = refs[:6]
    pos = 6
    s0_ref = None
    if has_state:
        s0_ref = refs[pos]
        pos += 1
    o_ref = refs[pos]
    pos += 1
    sn_ref = None
    if emit_state:
        sn_ref = refs[pos]
        pos += 1
    of_ref, st_ref = refs[pos], refs[pos + 1]

    C = HGRN_BLOCK
    n_chunks = seq // C
    row8 = lax.broadcasted_iota(jnp.int32, (SUBLANES, A_HEAD_DIM), 0)
    ti = lax.broadcasted_iota(jnp.int32, (C, C), 0)
    si = lax.broadcasted_iota(jnp.int32, (C, C), 1)
    norm_g = tab_ref[6:7, :]
    neg_inf = jnp.float32(-jnp.inf)

    for rev in (False, True):
        z_ref = zb_ref if rev else zf_ref
        base = 3 if rev else 0
        log_lb = tab_ref[base:base + 1, :]
        log_1mlb = tab_ref[base + 1:base + 2, :]
        one_mlb = tab_ref[base + 2:base + 3, :]
        tri = jnp.where((ti <= si) if rev else (ti >= si), 1.0, 0.0).astype(BF16)
        if has_state:
            st_ref[...] = s0_ref[1 if rev else 0].T
        else:
            st_ref[...] = jnp.zeros_like(st_ref)

        def body(ci, carry, rev=rev, z_ref=z_ref, log_lb=log_lb, log_1mlb=log_1mlb, one_mlb=one_mlb, tri=tri):
            c = (n_chunks - 1 - ci) if rev else ci
            rows = pl.ds(pl.multiple_of(c * C, C), C)
            qr = q_ref[rows, :]
            q = qr * _sigmoid(qr)
            v = i_ref[rows, :]
            z = z_ref[rows, :]
            e = jnp.exp(-jnp.abs(z))
            r = 1.0 / (1.0 + e)
            log_sig = jnp.minimum(z, 0.0) + jnp.log(r)
            k = one_mlb * (jnp.where(z >= 0.0, e, 1.0) * r)
            cc = log_1mlb + log_sig
            logf = jnp.maximum(log_lb, cc) + jnp.log1p(jnp.exp(-jnp.abs(log_lb - cc)))
            hi, mid, lo = _split3(logf)
            pieces = jnp.dot(tri, jnp.concatenate([hi, mid, lo], axis=1), preferred_element_type=F32)
            b = (pieces[:, :A_HEAD_DIM] + pieces[:, A_HEAD_DIM:2 * A_HEAD_DIM]) + pieces[:, 2 * A_HEAD_DIM:]
            b_end = b[0:1, :] if rev else b[C - 1:C, :]
            st = st_ref[...]
            qt = (q * jnp.exp(b)).astype(BF16)
            o = lax.dot_general(qt, st.astype(BF16), (((1,), (1,)), ((), ())), preferred_element_type=F32)
            halves = [o[0:SUBLANES, :], o[SUBLANES:C, :]]
            qh = [q[0:SUBLANES, :], q[SUBLANES:C, :]]
            bh = [b[0:SUBLANES, :], b[SUBLANES:C, :]]
            for s in range(C):
                bs, ks, vs = b[s:s + 1, :], k[s:s + 1, :], v[s:s + 1, :]
                for h in range(2):
                    lo_row, hi_row = h * SUBLANES, h * SUBLANES + SUBLANES - 1
                    if rev:
                        if lo_row > s:
                            continue
                        full = hi_row <= s
                    else:
                        if hi_row < s:
                            continue
                        full = lo_row >= s
                    d = bh[h] - bs
                    if not full:
                        keep = (row8 + lo_row <= s) if rev else (row8 + lo_row >= s)
                        d = jnp.where(keep, d, neg_inf)
                    p = (qh[h] * ks) * jnp.exp(d)
                    halves[h] = halves[h] + jnp.sum(p, axis=-1, keepdims=True) * vs
            o = jnp.concatenate(halves, axis=0)
            kt = (k * jnp.exp(b_end - b)).astype(BF16)
            upd = lax.dot_general(v.astype(BF16), kt, (((0,), (0,)), ((), ())), preferred_element_type=F32)
            st_ref[...] = st * jnp.exp(b_end) + upd
            if rev:
                tot = of_ref[rows, :] + o
                nrm = tot * lax.rsqrt(jnp.mean(tot * tot, axis=-1, keepdims=True) + LN_EPS) * norm_g
                gr = g_ref[rows, :]
                o_ref[rows, :] = (nrm * (gr * _sigmoid(gr))).astype(o_ref.dtype)
            else:
                of_ref[rows, :] = o
            return carry

        lax.fori_loop(0, n_chunks, body, 0)
        if emit_state:
            sn_ref[1 if rev else 0] = st_ref[...].T


def _hgrn(proj, tab, state, layer_j, row_block0, n_seq, seq, emit_state):
    has_state = state is not None
    col = lambda k: (lambda b, h: (row_block0 + b, k * A_HEADS + h))
    in_specs = [pl.BlockSpec((seq, A_HEAD_DIM), col(k)) for k in (0, 1, 2, 3, 4)]
    in_specs.append(pl.BlockSpec((SUBLANES, A_HEAD_DIM), lambda b, h: (0, h)))
    args = [proj] * 5 + [tab]
    if has_state:
        in_specs.append(pl.BlockSpec((None, None, 2, None, A_HEAD_DIM, A_HEAD_DIM),
                                     lambda b, h: (b, layer_j, 0, h, 0, 0)))
        args.append(state)
    out_specs = [pl.BlockSpec((seq, A_HEAD_DIM), lambda b, h: (b, h))]
    out_shape = [jax.ShapeDtypeStruct((n_seq * seq, A_WIDTH), BF16)]
    if emit_state:
        out_specs.append(pl.BlockSpec((None, 2, None, A_HEAD_DIM, A_HEAD_DIM), lambda b, h: (b, 0, h, 0, 0)))
        out_shape.append(jax.ShapeDtypeStruct((n_seq, 2, A_HEADS, A_HEAD_DIM, A_HEAD_DIM), F32))
    outs = pl.pallas_call(
        functools.partial(_hgrn_kernel, seq=seq, has_state=has_state, emit_state=emit_state),
        grid=(n_seq, A_HEADS),
        in_specs=in_specs,
        out_specs=out_specs,
        out_shape=out_shape,
        scratch_shapes=[pltpu.VMEM((seq, A_HEAD_DIM), F32), pltpu.VMEM((A_HEAD_DIM, A_HEAD_DIM), F32)],
        compiler_params=_cparams(2),
        name="hgrn_scan",
    )(*args)
    return outs


def _gelu_tanh(x):
    return 0.5 * x * (1.0 + jnp.tanh(math.sqrt(2.0 / math.pi) * (x + 0.044715 * (x * x * x))))


def _sgu_kernel(u_ref, v_ref, lng_ref, lnb_ref, w_ref, bias_ref, o_ref):
    v = _gelu_tanh(v_ref[...])
    mu = jnp.mean(v, axis=-1, keepdims=True)
    vc = v - mu
    var = jnp.mean(vc * vc, axis=-1, keepdims=True)
    vn = (vc * lax.rsqrt(var + LN_EPS) * lng_ref[...] + lnb_ref[...]).astype(BF16)
    for g in range(B_GROUPS):
        sl = slice(g * SGU_CHUNK, (g + 1) * SGU_CHUNK)
        mixed = jnp.dot(w_ref[g], vn[:, sl], preferred_element_type=F32) + bias_ref[:, sl]
        o_ref[:, sl] = (_gelu_tanh(u_ref[:, sl]) * mixed).astype(o_ref.dtype)


def _sgu(proj, ln_g, ln_b, w_s, bias_full):
    rows = proj.shape[0]
    ublk = 5 * A_WIDTH // B_WIDTH
    return pl.pallas_call(
        _sgu_kernel,
        grid=(rows // SGU_CHUNK,),
        in_specs=[pl.BlockSpec((SGU_CHUNK, B_WIDTH), lambda r: (r, ublk)),
                  pl.BlockSpec((SGU_CHUNK, B_WIDTH), lambda r: (r, ublk + 1)),
                  pl.BlockSpec((1, B_WIDTH), lambda r: (0, 0)),
                  pl.BlockSpec((1, B_WIDTH), lambda r: (0, 0)),
                  pl.BlockSpec((B_GROUPS, SGU_CHUNK, SGU_CHUNK), lambda r: (0, 0, 0)),
                  pl.BlockSpec((SGU_CHUNK, B_WIDTH), lambda r: (0, 0))],
        out_specs=pl.BlockSpec((SGU_CHUNK, B_WIDTH), lambda r: (r, 0)),
        out_shape=jax.ShapeDtypeStruct((rows, B_WIDTH), BF16),
        compiler_params=_cparams(1),
        name="sgu",
    )(proj, proj, ln_g, ln_b, w_s, bias_full)


def _attn_kernel(*refs, with_cache):
    if with_cache:
        qn_ref, qp_ref, kv_ref, kpe_ref, kvc_ref, kpec_ref, o_ref = refs
    else:
        qn_ref, qp_ref, kv_ref, kpe_ref, o_ref = refs
    scale = 1.0 / math.sqrt(NOPE_DIM + ROPE_DIM)
    nt = (((1,), (1,)), ((), ()))
    for i in range(2):
        qn = qn_ref[:, i * NOPE_DIM:(i + 1) * NOPE_DIM]
        qp = qp_ref[:, i * ROPE_DIM:(i + 1) * ROPE_DIM]
        c0 = i * (NOPE_DIM + V_DIM)
        s1 = (lax.dot_general(qn, kv_ref[:, c0:c0 + NOPE_DIM], nt, preferred_element_type=F32)
              + lax.dot_general(qp, kpe_ref[...], nt, preferred_element_type=F32)) * scale
        m = jnp.max(s1, axis=-1, keepdims=True)
        if with_cache:
            s2 = (lax.dot_general(qn, kvc_ref[:, c0:c0 + NOPE_DIM], nt, preferred_element_type=F32)
                  + lax.dot_general(qp, kpec_ref[...], nt, preferred_element_type=F32)) * scale
            m = jnp.maximum(m, jnp.max(s2, axis=-1, keepdims=True))
        p1 = jnp.exp(s1 - m)
        l = jnp.sum(p1, axis=-1, keepdims=True)
        acc = jnp.dot(p1.astype(BF16), kv_ref[:, c0 + NOPE_DIM:c0 + NOPE_DIM + V_DIM], preferred_element_type=F32)
        if with_cache:
            p2 = jnp.exp(s2 - m)
            l = l + jnp.sum(p2, axis=-1, keepdims=True)
            acc = acc + jnp.dot(p2.astype(BF16), kvc_ref[:, c0 + NOPE_DIM:c0 + NOPE_DIM + V_DIM],
                                preferred_element_type=F32)
        o_ref[:, i * V_DIM:(i + 1) * V_DIM] = (acc / l).astype(o_ref.dtype)


def _attn(qn, qp, kv, kpe, row0, n_seq, seq, tq, cache=None):
    nq = seq // tq
    qb0, kb0 = row0 // tq, row0 // seq
    pair = 2 * (NOPE_DIM + V_DIM)
    in_specs = [pl.BlockSpec((tq, 2 * NOPE_DIM), lambda b, h, i: (qb0 + b * nq + i, h)),
                pl.BlockSpec((tq, 2 * ROPE_DIM), lambda b, h, i: (qb0 + b * nq + i, h)),
                pl.BlockSpec((seq, pair), lambda b, h, i: (kb0 + b, h)),
                pl.BlockSpec((seq, ROPE_DIM), lambda b, h, i: (kb0 + b, 0))]
    args = [qn, qp, kv, kpe]
    if cache is not None:
        kvc, kpec, past = cache
        in_specs += [pl.BlockSpec((past, pair), lambda b, h, i: (b, h)),
                     pl.BlockSpec((past, ROPE_DIM), lambda b, h, i: (b, 0))]
        args += [kvc, kpec]
    return pl.pallas_call(
        functools.partial(_attn_kernel, with_cache=cache is not None),
        grid=(n_seq, C_HEADS // 2, nq),
        in_specs=in_specs,
        out_specs=pl.BlockSpec((tq, 2 * V_DIM), lambda b, h, i: (b * nq + i, h)),
        out_shape=jax.ShapeDtypeStruct((n_seq * seq, C_HEADS * V_DIM), BF16),
        compiler_params=_cparams(3),
        name="mla_attn",
    )(*args)


def _moe_up_kernel(be_ref, nu_ref, x_ref, wg_ref, wu_ref, h_ref, wgb_ref, wub_ref):
    i = pl.program_id(1)
    fresh = jnp.logical_or(i == 0, be_ref[i] != be_ref[jnp.maximum(i - 1, 0)])

    @pl.when(fresh)
    def _():
        wgb_ref[...] = wg_ref[...].astype(BF16)
        wub_ref[...] = wu_ref[...].astype(BF16)

    @pl.when(i < nu_ref[0])
    def _():
        x = x_ref[...]
        a = jnp.dot(x, wgb_ref[...], preferred_element_type=F32)
        b = jnp.dot(x, wub_ref[...], preferred_element_type=F32)
        h_ref[...] = (a * _sigmoid(a) * b).astype(h_ref.dtype)

    @pl.when(i >= nu_ref[0])
    def _():
        h_ref[...] = jnp.zeros_like(h_ref)


def _moe_down_kernel(be_ref, nu_ref, h_ref, wd_ref, y_ref, wdb_ref):
    i = pl.program_id(1)
    fresh = jnp.logical_or(i == 0, be_ref[i] != be_ref[jnp.maximum(i - 1, 0)])

    @pl.when(fresh)
    def _():
        wdb_ref[...] = wd_ref[...].astype(BF16)

    @pl.when(i < nu_ref[0])
    def _():
        y_ref[...] = jnp.dot(h_ref[...], wdb_ref[...], preferred_element_type=F32)

    @pl.when(i >= nu_ref[0])
    def _():
        y_ref[...] = jnp.zeros_like(y_ref)


def _moe_ffn(xp, blk_expert, n_used, w_gate, w_up, w_down):
    n_rows, D = xp.shape
    FF = w_gate.shape[-1]
    n_blk = n_rows // MOE_TM
    h = pl.pallas_call(
        _moe_up_kernel,
        grid_spec=pltpu.PrefetchScalarGridSpec(
            num_scalar_prefetch=2,
            grid=(FF // MOE_TF, n_blk),
            in_specs=[pl.BlockSpec((MOE_TM, D), lambda f, i, be, nu: (i, 0)),
                      pl.BlockSpec((None, D, MOE_TF), lambda f, i, be, nu: (be[i], 0, f)),
                      pl.BlockSpec((None, D, MOE_TF), lambda f, i, be, nu: (be[i], 0, f))],
            out_specs=pl.BlockSpec((MOE_TM, MOE_TF), lambda f, i, be, nu: (i, f)),
            scratch_shapes=[pltpu.VMEM((D, MOE_TF), BF16), pltpu.VMEM((D, MOE_TF), BF16)]),
        out_shape=jax.ShapeDtypeStruct((n_rows, FF), BF16),
        compiler_params=_cparams(2),
        name="moe_up",
    )(blk_expert, n_used, xp, w_gate, w_up)
    return pl.pallas_call(
        _moe_down_kernel,
        grid_spec=pltpu.PrefetchScalarGridSpec(
            num_scalar_prefetch=2,
            grid=(D // MOE_TN, n_blk),
            in_specs=[pl.BlockSpec((MOE_TM, FF), lambda n, i, be, nu: (i, 0)),
                      pl.BlockSpec((None, FF, MOE_TN), lambda n, i, be, nu: (be[i], 0, n))],
            out_specs=pl.BlockSpec((MOE_TM, MOE_TN), lambda n, i, be, nu: (i, n)),
            scratch_shapes=[pltpu.VMEM((FF, MOE_TN), BF16)]),
        out_shape=jax.ShapeDtypeStruct((n_rows, D), F32),
        compiler_params=_cparams(2),
        name="moe_down",
    )(blk_expert, n_used, h, w_down)


def _moe(x32, w_router_pad, b_router, w_gate, w_up, w_down):
    n_tok = x32.shape[0]
    xb = x32.astype(BF16)
    logits = _router_mm(x32, w_router_pad)[:, :N_EXPERTS]
    scores = jax.nn.sigmoid(logits)
    sel = (scores + b_router.astype(F32)).reshape(n_tok, N_GROUPS, EXPERTS_PER_GROUP)
    group_score = lax.top_k(sel, GROUP_SCORE_TOPK)[0].sum(-1)
    g_idx = jnp.argmax(group_score, axis=-1)
    sel_g = jnp.take_along_axis(sel, g_idx[:, None, None], axis=1)[:, 0]
    _, local = lax.top_k(sel_g, TOP_K)
    expert_idx = (g_idx[:, None] * EXPERTS_PER_GROUP + local).astype(jnp.int32)
    gate = jnp.take_along_axis(scores, expert_idx, axis=1)
    gate = gate / jnp.sum(gate, axis=-1, keepdims=True)

    flat_e = expert_idx.reshape(-1)
    n_flat = n_tok * TOP_K
    onehot = (flat_e[:, None] == jnp.arange(N_EXPERTS, dtype=jnp.int32)[None, :]).astype(jnp.int32)
    csum = jnp.cumsum(onehot, axis=0)
    rank = jnp.take_along_axis(csum, flat_e[:, None], axis=1)[:, 0] - 1
    counts = csum[-1]
    padded = (counts + MOE_TM - 1) // MOE_TM * MOE_TM
    pad_end = jnp.cumsum(padded)
    pad_start = pad_end - padded
    pos = pad_start[flat_e] + rank
    n_blk = (n_flat + N_EXPERTS * (MOE_TM - 1) + MOE_TM - 1) // MOE_TM
    n_rows = n_blk * MOE_TM
    blk_expert = jnp.minimum(jnp.searchsorted(pad_end, jnp.arange(n_blk, dtype=jnp.int32) * MOE_TM, side='right'),
                             N_EXPERTS - 1).astype(jnp.int32)
    n_used = (pad_end[-1:] // MOE_TM).astype(jnp.int32)
    src = jnp.zeros((n_rows,), jnp.int32).at[pos].set(jnp.arange(n_flat, dtype=jnp.int32) // TOP_K,
                                                       unique_indices=True)
    xp = jnp.take(xb, src, axis=0)
    yp = _moe_ffn(xp, blk_expert, n_used, w_gate, w_up, w_down)
    pos2 = pos.reshape(n_tok, TOP_K)
    return yp[pos2[:, 0]] * gate[:, 0:1] + yp[pos2[:, 1]] * gate[:, 1:2]


def _layernorm(x, g, b):
    mu = jnp.mean(x, axis=-1, keepdims=True)
    var = jnp.mean(jnp.square(x - mu), axis=-1, keepdims=True)
    return (x - mu) * lax.rsqrt(var + LN_EPS) * g + b


def _rmsnorm(x, g):
    return x * lax.rsqrt(jnp.mean(x * x, axis=-1, keepdims=True) + LN_EPS) * g


def _rot_cols(w):
    shp = w.shape
    w4 = w.reshape(shp[:-1] + (shp[-1] // 32, 2, 16))
    return jnp.stack([-w4[..., 1, :], w4[..., 0, :]], axis=-2).reshape(shp)


def _rope_tables(n_prompt_rows, n_seq, seq):
    t = jnp.arange(seq)
    row = (t // GRID_W).astype(F32)
    colp = (t % GRID_W).astype(F32)
    n_freq = ROPE_DIM // 4
    inv = ROPE_THETA ** (-jnp.arange(n_freq, dtype=F32) / n_freq)
    ang = jnp.stack([row[:, None] * inv, colp[:, None] * inv], axis=1)
    cos = jnp.broadcast_to(jnp.cos(ang)[:, :, None, :], (seq, 2, 2, n_freq)).reshape(seq, ROPE_DIM)
    sin = jnp.broadcast_to(jnp.sin(ang)[:, :, None, :], (seq, 2, 2, n_freq)).reshape(seq, ROPE_DIM)
    cos = jnp.concatenate([jnp.ones((n_prompt_rows, ROPE_DIM), F32), jnp.tile(cos, (n_seq, 1))], axis=0)
    sin = jnp.concatenate([jnp.zeros((n_prompt_rows, ROPE_DIM), F32), jnp.tile(sin, (n_seq, 1))], axis=0)
    return cos, sin


def kernel(x_prompt, x_sample, state_hgrn, cache_ckv, cache_kpe, c, c_ctx, w_ada, b_ada, ln_g, ln_b, w_in_ab, hgrn_lb, hgrn_norm_g, sgu_ln_g, sgu_ln_b, sgu_w, sgu_b, w_out_ab, w_in_c, q_norm_g, kv_norm_g, w_uq, w_ukv, w_o_c, w_router, b_router, w_gate_e, w_up_e, w_down_e):
    n_p, s_p, D = x_prompt.shape
    n_s, s_s, _ = x_sample.shape
    past = cache_ckv.shape[2]
    rows_p, rows_s = n_p * s_p, n_s * s_s
    rows = rows_p + rows_s
    assert rows_p % COND_ROWS == 0 and s_s == COND_ROWS
    n_grp = rows // COND_ROWS
    grp_p = rows_p // COND_ROWS

    h = jnp.concatenate([x_prompt.reshape(rows_p, D), x_sample.reshape(rows_s, D)], axis=0).reshape(n_grp, COND_ROWS, D)

    cond = jnp.concatenate([jnp.broadcast_to(c_ctx[None, :], (grp_p, D)), c], axis=0)
    cond_rows = -(-n_grp // 16) * 16
    cond = jnp.pad(jax.nn.silu(cond.astype(F32)), ((0, cond_rows - n_grp), (0, 0))).astype(BF16)

    p = jax.nn.softmax(hgrn_lb.astype(F32), axis=0)
    lbs = jnp.cumsum(p, axis=0) - p[0:1]

    cos_t, sin_t = _rope_tables(rows_p, n_s, s_s)
    w_router_pad = jnp.pad(w_router, ((0, 0), (0, 128 - N_EXPERTS)))

    hg_states, ckv_list, kpe_list = [], [], []
    for layer in range(DEPTH):
        j = layer // 2
        mod = _ada_mm(cond, w_ada[layer], b_ada[layer][None, :])[:n_grp]
        sh1, sc1, g1, sh2, sc2, g2 = [m[:, None, :] for m in jnp.split(mod, ADA_MULT, axis=-1)]
        a = (h * (1.0 + sc1) + sh1).astype(BF16).reshape(rows, D)
        if layer % 2 == 0:
            proj = _mm(a, w_in_ab[j].astype(BF16), F32)
            lb = lbs[j]
            tab = jnp.concatenate([jnp.log(lb[0:1]), jnp.log1p(-lb[0:1]), 1.0 - lb[0:1],
                                   jnp.log(lb[1:2]), jnp.log1p(-lb[1:2]), 1.0 - lb[1:2],
                                   hgrn_norm_g[j][None, :], jnp.zeros((1, A_WIDTH), F32)], axis=0)
            oa_p, st_p = _hgrn(proj, tab, None, j, 0, n_p, s_p, True)
            (oa_s,) = _hgrn(proj, tab, state_hgrn, j, rows_p // s_s, n_s, s_s, False)
            hg_states.append(st_p)
            bias_full = jnp.repeat(sgu_b[j].T, SGU_CHUNK, axis=1)
            o_s = _sgu(proj, sgu_ln_g[j][None, :], sgu_ln_b[j][None, :], sgu_w[j].astype(BF16), bias_full)
            mix_in = jnp.concatenate([jnp.concatenate([oa_p, oa_s], axis=0), o_s], axis=1)
            m = _mm(mix_in, w_out_ab[j].astype(BF16), F32)
        else:
            w_kpe = w_in_c[j][:, Q_LORA + KV_LORA:]
            n_c = Q_LORA + KV_LORA + 2 * ROPE_DIM
            w_in_ext = jnp.concatenate([w_in_c[j], _rot_cols(w_kpe), jnp.zeros((D, 2048 - n_c), F32)], axis=1)
            pc = _mm(a, w_in_ext.astype(BF16), F32)
            cq = _rmsnorm(pc[:, :Q_LORA], q_norm_g[j]).astype(BF16)
            ckv = _rmsnorm(pc[:, Q_LORA:Q_LORA + KV_LORA], kv_norm_g[j])
            kpe_raw = pc[:, Q_LORA + KV_LORA:Q_LORA + KV_LORA + ROPE_DIM]
            kpe_rot = pc[:, Q_LORA + KV_LORA + ROPE_DIM:n_c]
            kpe = (kpe_raw * cos_t + kpe_rot * sin_t).astype(BF16)
            ckv_list.append(ckv[:rows_p].reshape(n_p, s_p, KV_LORA))
            kpe_list.append(kpe_raw[:rows_p].reshape(n_p, s_p, ROPE_DIM))
            wq = w_uq[j].reshape(Q_LORA, C_HEADS, NOPE_DIM + ROPE_DIM)
            wq_n = wq[:, :, :NOPE_DIM].reshape(Q_LORA, C_HEADS * NOPE_DIM)
            wq_p = wq[:, :, NOPE_DIM:].reshape(Q_LORA, C_HEADS * ROPE_DIM)
            wq_ext = jnp.concatenate([wq_n, wq_p, _rot_cols(wq_p)], axis=1).astype(BF16)
            qx = _mm(cq, wq_ext, F32)
            n_n, n_r = C_HEADS * NOPE_DIM, C_HEADS * ROPE_DIM
            qn = qx[:, :n_n].astype(BF16)
            cos_h, sin_h = jnp.tile(cos_t, (1, C_HEADS)), jnp.tile(sin_t, (1, C_HEADS))
            qp = (qx[:, n_n:n_n + n_r] * cos_h + qx[:, n_n + n_r:] * sin_h).astype(BF16)
            w_ukv_b = w_ukv[j].astype(BF16)
            kv = _mm(ckv.astype(BF16), w_ukv_b, BF16)
            kvc = _mm(cache_ckv[:, j].reshape(n_s * past, KV_LORA).astype(BF16), w_ukv_b, BF16)
            kpec = cache_kpe[:, j].reshape(n_s * past, ROPE_DIM).astype(BF16)
            o_p = _attn(qn, qp, kv, kpe, 0, n_p, s_p, s_p)
            o_s = _attn(qn, qp, kv, kpe, rows_p, n_s, s_s, 256, cache=(kvc, kpec, past))
            m = _mm(jnp.concatenate([o_p, o_s], axis=0), w_o_c[j].astype(BF16), F32)
        h = _layernorm(DEEPNORM_ALPHA * h + g1 * m.reshape(n_grp, COND_ROWS, D), ln_g[layer, 0], ln_b[layer, 0])
        bmod = (h * (1.0 + sc2) + sh2).reshape(rows, D)
        y = _moe(bmod, w_router_pad, b_router, w_gate_e[layer], w_up_e[layer], w_down_e[layer])
        h = _layernorm(DEEPNORM_ALPHA * h + g2 * y.reshape(n_grp, COND_ROWS, D), ln_g[layer, 1], ln_b[layer, 1])

    h = h.reshape(rows, D)
    y_prompt = h[:rows_p].reshape(n_p, s_p, D)
    y_sample = h[rows_p:].reshape(n_s, s_s, D)
    return (y_prompt, y_sample, jnp.stack(hg_states, axis=1), jnp.stack(ckv_list, axis=1), jnp.stack(kpe_list, axis=1))
```

```python
import functools
import math

import jax
import jax.numpy as jnp
from jax import lax
from jax.experimental import pallas as pl
from jax.experimental.pallas import tpu as pltpu

F32 = jnp.float32
BF16 = jnp.bfloat16

D_MODEL = 4096
DEPTH = 4
GRID_W = 64
A_WIDTH = D_MODEL // 2
A_HEAD_DIM = 128
A_HEADS = A_WIDTH // A_HEAD_DIM
B_WIDTH = D_MODEL // 2
SGU_CHUNK = 128
B_GROUPS = 16
C_HEADS = D_MODEL // 128
Q_LORA = D_MODEL // 4
KV_LORA = 512
NOPE_DIM = 128
ROPE_DIM = 64
V_DIM = 128
ROPE_THETA = 10000.0
N_EXPERTS = 48
N_GROUPS = 8
EXPERTS_PER_GROUP = N_EXPERTS // N_GROUPS
TOP_K = 2
EXPERT_FF = D_MODEL // 4
ADA_MULT = 6
DEEPNORM_ALPHA = (2.0 * DEPTH) ** 0.25
LN_EPS = 1e-6

LANES = 128
SUBLANES = 8
COND_ROWS = 4096
HGRN_BLOCK = 16
HGRN_HEADS = 8
HGRN_ROWS = 512
LN_ROWS = 256
ROUTER_ROWS = N_GROUPS * SUBLANES
ATTN_TQ = 512
MOE_TM = 512
MOE_TF = 256
MOE_TN = 1024
VMEM_LIMIT = 56 * 1024 * 1024


def _cparams(n_axes):
    return pltpu.CompilerParams(dimension_semantics=("arbitrary",) * n_axes, vmem_limit_bytes=VMEM_LIMIT)


def _split2(a):
    hi = a.astype(BF16)
    return hi, (a - hi.astype(F32)).astype(BF16)


def _split3(a):
    hi = a.astype(BF16)
    r1 = a - hi.astype(F32)
    mid = r1.astype(BF16)
    lo = (r1 - mid.astype(F32)).astype(BF16)
    return hi, mid, lo


def _sigmoid(x):
    return 1.0 / (1.0 + jnp.exp(-x))


def _mm_kernel(x_ref, w_ref, o_ref):
    o_ref[...] = jnp.dot(x_ref[...], w_ref[...], preferred_element_type=F32).astype(o_ref.dtype)


def _mm(x, w, out_dtype, tm=512, tn=1024):
    M, K = x.shape
    N = w.shape[1]
    tm, tn = min(tm, M), min(tn, N)
    assert M % tm == 0 and N % tn == 0
    return pl.pallas_call(
        _mm_kernel,
        grid=(N // tn, M // tm),
        in_specs=[pl.BlockSpec((tm, K), lambda n, m: (m, 0)),
                  pl.BlockSpec((K, tn), lambda n, m: (0, n))],
        out_specs=pl.BlockSpec((tm, tn), lambda n, m: (m, n)),
        out_shape=jax.ShapeDtypeStruct((M, N), out_dtype),
        compiler_params=_cparams(2),
        name="dense_mm",
    )(x, w)


def _mm2_kernel(x1_ref, x2_ref, w1_ref, w2_ref, o_ref):
    o_ref[...] = (jnp.dot(x1_ref[...], w1_ref[...], preferred_element_type=F32)
                  + jnp.dot(x2_ref[...], w2_ref[...], preferred_element_type=F32)).astype(o_ref.dtype)


def _mm2(x1, x2, w, out_dtype, tm=512, tn=1024):
    M, K1 = x1.shape
    K2 = x2.shape[1]
    N = w.shape[1]
    assert K1 == K2 and w.shape[0] == K1 + K2
    return pl.pallas_call(
        _mm2_kernel,
        grid=(N // tn, M // tm),
        in_specs=[pl.BlockSpec((tm, K1), lambda n, m: (m, 0)),
                  pl.BlockSpec((tm, K2), lambda n, m: (m, 0)),
                  pl.BlockSpec((K1, tn), lambda n, m: (0, n)),
                  pl.BlockSpec((K2, tn), lambda n, m: (1, n))],
        out_specs=pl.BlockSpec((tm, tn), lambda n, m: (m, n)),
        out_shape=jax.ShapeDtypeStruct((M, N), out_dtype),
        compiler_params=_cparams(2),
        name="dense_mm2",
    )(x1, x2, w, w)


def _ada_kernel(x_ref, w_ref, b_ref, o_ref):
    o_ref[...] = jnp.dot(x_ref[...], w_ref[...].astype(BF16), preferred_element_type=F32) + b_ref[...]


def _ada_mm(x, w_all, b_all, layer, tn=1024):
    R, K = x.shape
    N = w_all.shape[2]
    return pl.pallas_call(
        _ada_kernel,
        grid=(N // tn,),
        in_specs=[pl.BlockSpec((R, K), lambda n: (0, 0)),
                  pl.BlockSpec((None, K, tn), lambda n: (layer, 0, n)),
                  pl.BlockSpec((None, 1, tn), lambda n: (layer, 0, n))],
        out_specs=pl.BlockSpec((R, tn), lambda n: (0, n)),
        out_shape=jax.ShapeDtypeStruct((R, N), F32),
        compiler_params=_cparams(1),
        name="ada_mm",
    )(x, w_all, b_all)


def _ln_kernel(*refs, mode):
    if mode == "first":
        h_ref, sc_ref, sh_ref, mod_ref = refs
        hn = h_ref[...]
    else:
        h_ref, m_ref, gate_ref, lng_ref, lnb_ref = refs[:5]
        x = DEEPNORM_ALPHA * h_ref[...] + gate_ref[...] * m_ref[...]
        mu = jnp.mean(x, axis=-1, keepdims=True)
        xc = x - mu
        var = jnp.mean(xc * xc, axis=-1, keepdims=True)
        hn = xc * lax.rsqrt(var + LN_EPS) * lng_ref[...] + lnb_ref[...]
        if mode == "last":
            refs[5][...] = hn
            return
        sc_ref, sh_ref = refs[5:7]
        if mode == "moe":
            wh_ref, wm_ref, hn_ref, mod_ref, lt_ref = refs[7:]
        else:
            hn_ref, mod_ref = refs[7:]
        hn_ref[...] = hn
    mod = hn * (1.0 + sc_ref[...]) + sh_ref[...]
    mod_ref[...] = mod.astype(mod_ref.dtype)
    if mode == "moe":
        xh, xm = _split2(mod)
        nt = (((1,), (1,)), ((), ()))
        dot = functools.partial(lax.dot_general, dimension_numbers=nt, preferred_element_type=F32)
        lt_ref[...] = dot(wh_ref[...], xh) + (dot(wh_ref[...], xm) + dot(wm_ref[...], xh))


def _ln_call(mode, h, m=None, gate=None, ln_g=None, ln_b=None, sc=None, sh=None, w_router=None):
    rows, D = h.shape
    tm = LN_ROWS
    grp = lambda i: (i * tm // COND_ROWS, 0, 0)
    row_spec = pl.BlockSpec((tm, D), lambda i: (i, 0))
    cond_spec = pl.BlockSpec((None, 1, D), grp)
    vec_spec = pl.BlockSpec((1, D), lambda i: (0, 0))
    args, in_specs = [h], [row_spec]
    if mode != "first":
        args += [m, gate, ln_g, ln_b]
        in_specs += [row_spec, cond_spec, vec_spec, vec_spec]
    if mode != "last":
        args += [sc, sh]
        in_specs += [cond_spec, cond_spec]
    out_shape, out_specs = [], []
    if mode != "first":
        out_shape.append(jax.ShapeDtypeStruct((rows, D), F32))
        out_specs.append(row_spec)
    if mode != "last":
        out_shape.append(jax.ShapeDtypeStruct((rows, D), BF16))
        out_specs.append(row_spec)
    if mode == "moe":
        args += list(w_router)
        in_specs += [pl.BlockSpec((ROUTER_ROWS, D), lambda i: (0, 0))] * 2
        out_shape.append(jax.ShapeDtypeStruct((ROUTER_ROWS, rows), F32))
        out_specs.append(pl.BlockSpec((ROUTER_ROWS, tm), lambda i: (0, i)))
    return pl.pallas_call(
        functools.partial(_ln_kernel, mode=mode),
        grid=(rows // tm,),
        in_specs=in_specs,
        out_specs=out_specs,
        out_shape=out_shape,
        compiler_params=_cparams(1),
        name="ln_" + mode,
    )(*args)


def _hgrn_kernel(*refs, rev, ts, heads, has_state, emit_state):
    refs = list(refs)
    q_ref, i_ref, z_ref = refs[:3]
    pos = 3
    if rev:
        g_ref, of_ref = refs[pos:pos + 2]
        pos += 2
    tab_ref = refs[pos]
    pos += 1
    s0_ref = None
    if has_state:
        s0_ref = refs[pos]
        pos += 1
    o_ref = refs[pos]
    pos += 1
    sn_ref = None
    if emit_state:
        sn_ref = refs[pos]
        pos += 1
    st_ref = refs[pos]

    t = pl.program_id(2)
    n_t = pl.num_programs(2)
    C = HGRN_BLOCK
    n_chunks = ts // C
    dh = A_HEAD_DIM

    @pl.when(t == 0)
    def _():
        for hd in range(heads):
            if has_state:
                st_ref[hd] = s0_ref[hd].T
            else:
                st_ref[hd] = jnp.zeros((dh, dh), F32)

    row8 = lax.broadcasted_iota(jnp.int32, (SUBLANES, dh), 0)
    ti = lax.broadcasted_iota(jnp.int32, (C, C), 0)
    si = lax.broadcasted_iota(jnp.int32, (C, C), 1)
    tri = jnp.where((ti <= si) if rev else (ti >= si), 1.0, 0.0).astype(BF16)
    neg_inf = jnp.float32(-jnp.inf)
    base = 3 if rev else 0

    def head_chunk(rows, hd):
        cols = slice(hd * dh, (hd + 1) * dh)
        log_lb = tab_ref[base:base + 1, cols]
        log_1mlb = tab_ref[base + 1:base + 2, cols]
        one_mlb = tab_ref[base + 2:base + 3, cols]
        qr = q_ref[rows, cols]
        q = qr * _sigmoid(qr)
        v = i_ref[rows, cols]
        z = z_ref[rows, cols]
        e = jnp.exp(-jnp.abs(z))
        r = 1.0 / (1.0 + e)
        log_sig = jnp.minimum(z, 0.0) + jnp.log(r)
        k = one_mlb * (jnp.where(z >= 0.0, e, 1.0) * r)
        cc = log_1mlb + log_sig
        logf = jnp.maximum(log_lb, cc) + jnp.log1p(jnp.exp(-jnp.abs(log_lb - cc)))
        hi, mid, lo = _split3(logf)
        pieces = jnp.dot(tri, jnp.concatenate([hi, mid, lo], axis=1), preferred_element_type=F32)
        b = (pieces[:, :dh] + pieces[:, dh:2 * dh]) + pieces[:, 2 * dh:]
        b_end = b[0:1, :] if rev else b[C - 1:C, :]
        st = st_ref[hd]
        qt = (q * jnp.exp(b)).astype(BF16)
        o = lax.dot_general(qt, st.astype(BF16), (((1,), (1,)), ((), ())), preferred_element_type=F32)
        halves = [o[0:SUBLANES, :], o[SUBLANES:C, :]]
        qh = [q[0:SUBLANES, :], q[SUBLANES:C, :]]
        bh = [b[0:SUBLANES, :], b[SUBLANES:C, :]]
        for s in range(C):
            bs, ks, vs = b[s:s + 1, :], k[s:s + 1, :], v[s:s + 1, :]
            for h in range(2):
                lo_row, hi_row = h * SUBLANES, h * SUBLANES + SUBLANES - 1
                if rev:
                    if lo_row > s:
                        continue
                    full = hi_row <= s
                else:
                    if hi_row < s:
                        continue
                    full = lo_row >= s
                d = bh[h] - bs
                if not full:
                    keep = (row8 + lo_row <= s) if rev else (row8 + lo_row >= s)
                    d = jnp.where(keep, d, neg_inf)
                p = (qh[h] * ks) * jnp.exp(d)
                halves[h] = halves[h] + jnp.sum(p, axis=-1, keepdims=True) * vs
        o = jnp.concatenate(halves, axis=0)
        kt = (k * jnp.exp(b_end - b)).astype(BF16)
        upd = lax.dot_general(v.astype(BF16), kt, (((0,), (0,)), ((), ())), preferred_element_type=F32)
        st_ref[hd] = st * jnp.exp(b_end) + upd
        if rev:
            tot = of_ref[rows, cols] + o
            nrm = tot * lax.rsqrt(jnp.mean(tot * tot, axis=-1, keepdims=True) + LN_EPS) * tab_ref[6:7, cols]
            gr = g_ref[rows, cols]
            o_ref[rows, cols] = (nrm * (gr * _sigmoid(gr))).astype(o_ref.dtype)
        else:
            o_ref[rows, cols] = o

    def body(ci, carry):
        c = (n_chunks - 1 - ci) if rev else ci
        rows = pl.ds(pl.multiple_of(c * C, C), C)
        for hd in range(heads):
            head_chunk(rows, hd)
        return carry

    lax.fori_loop(0, n_chunks, body, 0)

    if emit_state:
        @pl.when(t == n_t - 1)
        def _():
            for hd in range(heads):
                sn_ref[hd] = st_ref[hd].T


def _hgrn_dir(proj, part, tab, state, layer_j, row0, n_seq, seq, rev, emit_state):
    has_state = state is not None
    G = HGRN_HEADS
    ts = min(HGRN_ROWS, seq)
    n_t = seq // ts
    hg_n = A_HEADS // G
    blk0 = row0 // ts
    tt = (lambda t: n_t - 1 - t) if rev else (lambda t: t)
    col = lambda k: (lambda b, hg, t: (blk0 + b * n_t + tt(t), k * hg_n + hg))
    in_specs = [pl.BlockSpec((ts, G * A_HEAD_DIM), col(k)) for k in (0, 1, 3 if rev else 2)]
    args = [proj, proj, proj]
    if rev:
        in_specs += [pl.BlockSpec((ts, G * A_HEAD_DIM), col(4)),
                     pl.BlockSpec((ts, G * A_HEAD_DIM), lambda b, hg, t: (b * n_t + tt(t), hg))]
        args += [proj, part]
    in_specs.append(pl.BlockSpec((SUBLANES, G * A_HEAD_DIM), lambda b, hg, t: (0, hg)))
    args.append(tab)
    if has_state:
        in_specs.append(pl.BlockSpec((None, None, None, G, A_HEAD_DIM, A_HEAD_DIM),
                                     lambda b, hg, t: (b, layer_j, 1 if rev else 0, hg, 0, 0)))
        args.append(state)
    out_specs = [pl.BlockSpec((ts, G * A_HEAD_DIM), lambda b, hg, t: (b * n_t + tt(t), hg))]
    out_shape = [jax.ShapeDtypeStruct((n_seq * seq, A_WIDTH), BF16 if rev else F32)]
    if emit_state:
        out_specs.append(pl.BlockSpec((None, G, A_HEAD_DIM, A_HEAD_DIM), lambda b, hg, t: (b, hg, 0, 0)))
        out_shape.append(jax.ShapeDtypeStruct((n_seq, A_HEADS, A_HEAD_DIM, A_HEAD_DIM), F32))
    return pl.pallas_call(
        functools.partial(_hgrn_kernel, rev=rev, ts=ts, heads=G, has_state=has_state, emit_state=emit_state),
        grid=(n_seq, hg_n, n_t),
        in_specs=in_specs,
        out_specs=out_specs,
        out_shape=out_shape,
        scratch_shapes=[pltpu.VMEM((G, A_HEAD_DIM, A_HEAD_DIM), F32)],
        compiler_params=_cparams(3),
        name="hgrn_bwd" if rev else "hgrn_fwd",
    )(*args)


def _hgrn(proj, tab, state, layer_j, row0, n_seq, seq, emit_state):
    fwd = _hgrn_dir(proj, None, tab, state, layer_j, row0, n_seq, seq, False, emit_state)
    bwd = _hgrn_dir(proj, fwd[0], tab, state, layer_j, row0, n_seq, seq, True, emit_state)
    if emit_state:
        return bwd[0], jnp.stack([fwd[1], bwd[1]], axis=1)
    return bwd[0], None


def _gelu_tanh(x):
    return 0.5 * x * (1.0 + jnp.tanh(math.sqrt(2.0 / math.pi) * (x + 0.044715 * (x * x * x))))


def _sgu_kernel(u_ref, v_ref, lng_ref, lnb_ref, w_ref, bias_ref, o_ref):
    v = _gelu_tanh(v_ref[...])
    mu = jnp.mean(v, axis=-1, keepdims=True)
    vc = v - mu
    var = jnp.mean(vc * vc, axis=-1, keepdims=True)
    vn = (vc * lax.rsqrt(var + LN_EPS) * lng_ref[...] + lnb_ref[...]).astype(BF16)
    for g in range(B_GROUPS):
        sl = slice(g * SGU_CHUNK, (g + 1) * SGU_CHUNK)
        mixed = jnp.dot(w_ref[g], vn[:, sl], preferred_element_type=F32) + bias_ref[:, sl]
        o_ref[:, sl] = (_gelu_tanh(u_ref[:, sl]) * mixed).astype(o_ref.dtype)


def _sgu(proj, ln_g, ln_b, w_s, bias_full):
    rows = proj.shape[0]
    ublk = 5 * A_WIDTH // B_WIDTH
    return pl.pallas_call(
        _sgu_kernel,
        grid=(rows // SGU_CHUNK,),
        in_specs=[pl.BlockSpec((SGU_CHUNK, B_WIDTH), lambda r: (r, ublk)),
                  pl.BlockSpec((SGU_CHUNK, B_WIDTH), lambda r: (r, ublk + 1)),
                  pl.BlockSpec((1, B_WIDTH), lambda r: (0, 0)),
                  pl.BlockSpec((1, B_WIDTH), lambda r: (0, 0)),
                  pl.BlockSpec((B_GROUPS, SGU_CHUNK, SGU_CHUNK), lambda r: (0, 0, 0)),
                  pl.BlockSpec((SGU_CHUNK, B_WIDTH), lambda r: (0, 0))],
        out_specs=pl.BlockSpec((SGU_CHUNK, B_WIDTH), lambda r: (r, 0)),
        out_shape=jax.ShapeDtypeStruct((rows, B_WIDTH), BF16),
        compiler_params=_cparams(1),
        name="sgu",
    )(proj, proj, ln_g, ln_b, w_s, bias_full)


def _attn_kernel(*refs, with_cache):
    if with_cache:
        qn_ref, qp_ref, kv_ref, kpe_ref, kvc_ref, kpec_ref, o_ref, vx_ref, vxc_ref = refs
    else:
        qn_ref, qp_ref, kv_ref, kpe_ref, o_ref, vx_ref = refs
    nt = (((1,), (1,)), ((), ()))
    pair = NOPE_DIM + V_DIM

    @pl.when(pl.program_id(2) == 0)
    def _():
        for i in range(2):
            vx_ref[i, :, 0:V_DIM] = kv_ref[:, i * pair + NOPE_DIM:(i + 1) * pair]
            vx_ref[i, :, V_DIM:2 * V_DIM] = jnp.ones((vx_ref.shape[1], V_DIM), BF16)
            if with_cache:
                vxc_ref[i, :, 0:V_DIM] = kvc_ref[:, i * pair + NOPE_DIM:(i + 1) * pair]
                vxc_ref[i, :, V_DIM:2 * V_DIM] = jnp.ones((vxc_ref.shape[1], V_DIM), BF16)

    for i in range(2):
        qn = qn_ref[:, i * NOPE_DIM:(i + 1) * NOPE_DIM]
        qp = qp_ref[:, i * ROPE_DIM:(i + 1) * ROPE_DIM]
        c0 = i * pair
        s1 = (lax.dot_general(qn, kv_ref[:, c0:c0 + NOPE_DIM], nt, preferred_element_type=F32)
              + lax.dot_general(qp, kpe_ref[...], nt, preferred_element_type=F32))
        m = jnp.max(s1, axis=-1, keepdims=True)
        if with_cache:
            s2 = (lax.dot_general(qn, kvc_ref[:, c0:c0 + NOPE_DIM], nt, preferred_element_type=F32)
                  + lax.dot_general(qp, kpec_ref[...], nt, preferred_element_type=F32))
            m = jnp.maximum(m, jnp.max(s2, axis=-1, keepdims=True))
        acc = jnp.dot(jnp.exp2(s1 - m).astype(BF16), vx_ref[i], preferred_element_type=F32)
        if with_cache:
            acc = acc + jnp.dot(jnp.exp2(s2 - m).astype(BF16), vxc_ref[i], preferred_element_type=F32)
        o_ref[:, i * V_DIM:(i + 1) * V_DIM] = (acc[:, 0:V_DIM] / acc[:, V_DIM:V_DIM + 1]).astype(o_ref.dtype)


def _attn(qn, qp, kv, kpe, row0, n_seq, seq, tq, cache=None):
    nq = seq // tq
    qb0, kb0 = row0 // tq, row0 // seq
    pair = 2 * (NOPE_DIM + V_DIM)
    in_specs = [pl.BlockSpec((tq, 2 * NOPE_DIM), lambda b, h, i: (qb0 + b * nq + i, h)),
                pl.BlockSpec((tq, 2 * ROPE_DIM), lambda b, h, i: (qb0 + b * nq + i, h)),
                pl.BlockSpec((seq, pair), lambda b, h, i: (kb0 + b, h)),
                pl.BlockSpec((seq, ROPE_DIM), lambda b, h, i: (kb0 + b, 0))]
    args = [qn, qp, kv, kpe]
    scratch = [pltpu.VMEM((2, seq, 2 * V_DIM), BF16)]
    if cache is not None:
        kvc, kpec, past = cache
        in_specs += [pl.BlockSpec((past, pair), lambda b, h, i: (b, h)),
                     pl.BlockSpec((past, ROPE_DIM), lambda b, h, i: (b, 0))]
        args += [kvc, kpec]
        scratch.append(pltpu.VMEM((2, past, 2 * V_DIM), BF16))
    return pl.pallas_call(
        functools.partial(_attn_kernel, with_cache=cache is not None),
        grid=(n_seq, C_HEADS // 2, nq),
        in_specs=in_specs,
        out_specs=pl.BlockSpec((tq, 2 * V_DIM), lambda b, h, i: (b * nq + i, h)),
        out_shape=jax.ShapeDtypeStruct((n_seq * seq, C_HEADS * V_DIM), BF16),
        scratch_shapes=scratch,
        compiler_params=_cparams(3),
        name="mla_attn",
    )(*args)


def _moe_up_kernel(be_ref, nu_ref, x_ref, wg_ref, wu_ref, h_ref, wgb_ref, wub_ref):
    i = pl.program_id(1)
    fresh = jnp.logical_or(i == 0, be_ref[i] != be_ref[jnp.maximum(i - 1, 0)])

    @pl.when(fresh)
    def _():
        wgb_ref[...] = wg_ref[...].astype(BF16)
        wub_ref[...] = wu_ref[...].astype(BF16)

    @pl.when(i < nu_ref[0])
    def _():
        x = x_ref[...]
        a = jnp.dot(x, wgb_ref[...], preferred_element_type=F32)
        b = jnp.dot(x, wub_ref[...], preferred_element_type=F32)
        h_ref[...] = (a * _sigmoid(a) * b).astype(h_ref.dtype)

    @pl.when(i >= nu_ref[0])
    def _():
        h_ref[...] = jnp.zeros_like(h_ref)


def _moe_down_kernel(be_ref, nu_ref, h_ref, wd_ref, y_ref, wdb_ref):
    i = pl.program_id(1)
    fresh = jnp.logical_or(i == 0, be_ref[i] != be_ref[jnp.maximum(i - 1, 0)])

    @pl.when(fresh)
    def _():
        wdb_ref[...] = wd_ref[...].astype(BF16)

    @pl.when(i < nu_ref[0])
    def _():
        y_ref[...] = jnp.dot(h_ref[...], wdb_ref[...], preferred_element_type=F32)

    @pl.when(i >= nu_ref[0])
    def _():
        y_ref[...] = jnp.zeros_like(y_ref)


def _moe_ffn(xp, blk_expert, n_used, w_gate, w_up, w_down, layer):
    n_rows, D = xp.shape
    FF = w_gate.shape[-1]
    n_blk = n_rows // MOE_TM
    h = pl.pallas_call(
        _moe_up_kernel,
        grid_spec=pltpu.PrefetchScalarGridSpec(
            num_scalar_prefetch=2,
            grid=(FF // MOE_TF, n_blk),
            in_specs=[pl.BlockSpec((MOE_TM, D), lambda f, i, be, nu: (i, 0)),
                      pl.BlockSpec((None, None, D, MOE_TF), lambda f, i, be, nu: (layer, be[i], 0, f)),
                      pl.BlockSpec((None, None, D, MOE_TF), lambda f, i, be, nu: (layer, be[i], 0, f))],
            out_specs=pl.BlockSpec((MOE_TM, MOE_TF), lambda f, i, be, nu: (i, f)),
            scratch_shapes=[pltpu.VMEM((D, MOE_TF), BF16), pltpu.VMEM((D, MOE_TF), BF16)]),
        out_shape=jax.ShapeDtypeStruct((n_rows, FF), BF16),
        compiler_params=_cparams(2),
        name="moe_up",
    )(blk_expert, n_used, xp, w_gate, w_up)
    return pl.pallas_call(
        _moe_down_kernel,
        grid_spec=pltpu.PrefetchScalarGridSpec(
            num_scalar_prefetch=2,
            grid=(D // MOE_TN, n_blk),
            in_specs=[pl.BlockSpec((MOE_TM, FF), lambda n, i, be, nu: (i, 0)),
                      pl.BlockSpec((None, None, FF, MOE_TN), lambda n, i, be, nu: (layer, be[i], 0, n))],
            out_specs=pl.BlockSpec((MOE_TM, MOE_TN), lambda n, i, be, nu: (i, n)),
            scratch_shapes=[pltpu.VMEM((FF, MOE_TN), BF16)]),
        out_shape=jax.ShapeDtypeStruct((n_rows, D), F32),
        compiler_params=_cparams(2),
        name="moe_down",
    )(blk_expert, n_used, h, w_down)


def _route(logits_t, b_router_t):
    n_tok = logits_t.shape[1]
    scores = jax.nn.sigmoid(logits_t).reshape(N_GROUPS, SUBLANES, n_tok)
    r_io = lax.broadcasted_iota(jnp.int32, (N_GROUPS, SUBLANES, n_tok), 1)
    g_io = lax.broadcasted_iota(jnp.int32, (N_GROUPS, n_tok), 0)
    sel = jnp.where(r_io < EXPERTS_PER_GROUP, scores + b_router_t.reshape(N_GROUPS, SUBLANES, 1), -jnp.inf)
    m1 = jnp.max(sel, axis=1)
    i1 = jnp.min(jnp.where(sel == m1[:, None, :], r_io, SUBLANES), axis=1)
    sel2 = jnp.where(r_io == i1[:, None, :], -jnp.inf, sel)
    m2 = jnp.max(sel2, axis=1)
    i2 = jnp.min(jnp.where(sel2 == m2[:, None, :], r_io, SUBLANES), axis=1)
    group_score = m1 + m2
    g_idx = jnp.min(jnp.where(group_score == jnp.max(group_score, axis=0, keepdims=True), g_io, N_GROUPS), axis=0)
    best = g_io == g_idx[None, :]
    l1 = jnp.sum(jnp.where(best, i1, 0), axis=0)
    l2 = jnp.sum(jnp.where(best, i2, 0), axis=0)
    local = jnp.stack([l1, l2], axis=0)
    pick = best[None, :, None, :] & (r_io[None] == local[:, None, None, :])
    s = jnp.sum(jnp.where(pick, scores[None], 0.0), axis=(1, 2))
    gate = s / jnp.sum(s, axis=0, keepdims=True)
    return (g_idx[None, :] * EXPERTS_PER_GROUP + local).astype(jnp.int32), gate


def _dispatch(expert_idx):
    n_tok = expert_idx.shape[1]
    n_flat = TOP_K * n_tok
    nb = n_flat // LANES
    e_flat = expert_idx.reshape(nb, LANES)
    onehot = e_flat[None] == lax.broadcasted_iota(jnp.int32, (N_EXPERTS, nb, LANES), 0)
    upper = (lax.broadcasted_iota(jnp.int32, (LANES, LANES), 0)
             <= lax.broadcasted_iota(jnp.int32, (LANES, LANES), 1)).astype(BF16)
    within = jnp.einsum('ebl,lm->ebm', onehot.astype(BF16), upper, preferred_element_type=F32)
    blk_cnt = within[:, :, LANES - 1]
    blk_end = jnp.cumsum(blk_cnt, axis=1)
    counts = blk_end[:, -1].astype(jnp.int32)
    padded = (counts + MOE_TM - 1) // MOE_TM * MOE_TM
    pad_end = jnp.cumsum(padded)
    pad_start = pad_end - padded
    offs = (blk_end - blk_cnt)[:, :, None] + pad_start.astype(F32)[:, None, None] + within - 1.0
    pos = jnp.sum(jnp.where(onehot, offs, 0.0), axis=0).astype(jnp.int32).reshape(n_flat)
    n_blk = (n_flat + N_EXPERTS * (MOE_TM - 1) + MOE_TM - 1) // MOE_TM
    blk_expert = jnp.minimum(jnp.searchsorted(pad_end, jnp.arange(n_blk, dtype=jnp.int32) * MOE_TM, side='right'),
                             N_EXPERTS - 1).astype(jnp.int32)
    n_used = (pad_end[-1:] // MOE_TM).astype(jnp.int32)
    tok = lax.rem(jnp.arange(n_flat, dtype=jnp.int32), n_tok)
    src = jnp.zeros((n_blk * MOE_TM,), jnp.int32).at[pos].set(tok, unique_indices=True)
    return pos.reshape(TOP_K, n_tok), blk_expert, n_used, src


def _moe(xb, logits_t, b_router_t, w_gate, w_up, w_down, layer):
    expert_idx, gate = _route(logits_t, b_router_t)
    pos, blk_expert, n_used, src = _dispatch(expert_idx)
    xp = jnp.take(xb, src, axis=0)
    yp = _moe_ffn(xp, blk_expert, n_used, w_gate, w_up, w_down, layer)
    return yp[pos[0]] * gate[0][:, None] + yp[pos[1]] * gate[1][:, None]


def _rmsnorm(x, g):
    return x * lax.rsqrt(jnp.mean(x * x, axis=-1, keepdims=True) + LN_EPS) * g


def _rot_cols(w):
    shp = w.shape
    w4 = w.reshape(shp[:-1] + (shp[-1] // 32, 2, 16))
    return jnp.stack([-w4[..., 1, :], w4[..., 0, :]], axis=-2).reshape(shp)


def _rope_tables(n_prompt_rows, n_seq, seq):
    t = jnp.arange(seq)
    row = (t // GRID_W).astype(F32)
    colp = (t % GRID_W).astype(F32)
    n_freq = ROPE_DIM // 4
    inv = ROPE_THETA ** (-jnp.arange(n_freq, dtype=F32) / n_freq)
    ang = jnp.stack([row[:, None] * inv, colp[:, None] * inv], axis=1)
    cos = jnp.broadcast_to(jnp.cos(ang)[:, :, None, :], (seq, 2, 2, n_freq)).reshape(seq, ROPE_DIM)
    sin = jnp.broadcast_to(jnp.sin(ang)[:, :, None, :], (seq, 2, 2, n_freq)).reshape(seq, ROPE_DIM)
    cos = jnp.concatenate([jnp.ones((n_prompt_rows, ROPE_DIM), F32), jnp.tile(cos, (n_seq, 1))], axis=0)
    sin = jnp.concatenate([jnp.zeros((n_prompt_rows, ROPE_DIM), F32), jnp.tile(sin, (n_seq, 1))], axis=0)
    return cos, sin


def _router_weights(w_router, b_router):
    D = w_router.shape[0]
    wt = w_router.T.reshape(N_GROUPS, EXPERTS_PER_GROUP, D)
    wt = jnp.pad(wt, ((0, 0), (0, SUBLANES - EXPERTS_PER_GROUP), (0, 0))).reshape(ROUTER_ROWS, D)
    bt = jnp.pad(b_router.astype(F32).reshape(N_GROUPS, EXPERTS_PER_GROUP),
                 ((0, 0), (0, SUBLANES - EXPERTS_PER_GROUP))).reshape(ROUTER_ROWS)
    return _split2(wt.astype(F32)), bt


def kernel(x_prompt, x_sample, state_hgrn, cache_ckv, cache_kpe, c, c_ctx, w_ada, b_ada, ln_g, ln_b, w_in_ab, hgrn_lb, hgrn_norm_g, sgu_ln_g, sgu_ln_b, sgu_w, sgu_b, w_out_ab, w_in_c, q_norm_g, kv_norm_g, w_uq, w_ukv, w_o_c, w_router, b_router, w_gate_e, w_up_e, w_down_e):
    n_p, s_p, D = x_prompt.shape
    n_s, s_s, _ = x_sample.shape
    past = cache_ckv.shape[2]
    rows_p, rows_s = n_p * s_p, n_s * s_s
    rows = rows_p + rows_s
    assert rows_p % COND_ROWS == 0 and s_s == COND_ROWS
    n_grp = rows // COND_ROWS
    grp_p = rows_p // COND_ROWS

    h = jnp.concatenate([x_prompt.reshape(rows_p, D), x_sample.reshape(rows_s, D)], axis=0)

    cond = jnp.concatenate([jnp.broadcast_to(c_ctx[None, :], (grp_p, D)), c], axis=0)
    cond_rows = -(-n_grp // 16) * 16
    cond = jnp.pad(jax.nn.silu(cond.astype(F32)), ((0, cond_rows - n_grp), (0, 0))).astype(BF16)
    b_ada3 = b_ada.reshape(DEPTH, 1, ADA_MULT * D)

    def modulation(layer):
        mod = _ada_mm(cond, w_ada, b_ada3, layer)[:n_grp]
        return [m[:, None, :] for m in jnp.split(mod, ADA_MULT, axis=-1)]

    p = jax.nn.softmax(hgrn_lb.astype(F32), axis=0)
    lbs = jnp.cumsum(p, axis=0) - p[0:1]

    cos_t, sin_t = _rope_tables(rows_p, n_s, s_s)
    w_router_t, b_router_t = _router_weights(w_router, b_router)
    q_scale = math.log2(math.e) / math.sqrt(NOPE_DIM + ROPE_DIM)

    hg_states, ckv_list, kpe_list = [], [], []
    sh1, sc1, g1, sh2, sc2, g2 = modulation(0)
    (a,) = _ln_call("first", h, sc=sc1, sh=sh1)
    for layer in range(DEPTH):
        j = layer // 2
        if layer % 2 == 0:
            proj = _mm(a, w_in_ab[j].astype(BF16), F32)
            lb = lbs[j]
            tab = jnp.concatenate([jnp.log(lb[0:1]), jnp.log1p(-lb[0:1]), 1.0 - lb[0:1],
                                   jnp.log(lb[1:2]), jnp.log1p(-lb[1:2]), 1.0 - lb[1:2],
                                   hgrn_norm_g[j][None, :], jnp.zeros((1, A_WIDTH), F32)], axis=0)
            oa_p, st_p = _hgrn(proj, tab, None, j, 0, n_p, s_p, True)
            oa_s, _ = _hgrn(proj, tab, state_hgrn, j, rows_p, n_s, s_s, False)
            hg_states.append(st_p)
            bias_full = jnp.repeat(sgu_b[j].T, SGU_CHUNK, axis=1)
            o_sgu = _sgu(proj, sgu_ln_g[j][None, :], sgu_ln_b[j][None, :], sgu_w[j].astype(BF16), bias_full)
            m = _mm2(jnp.concatenate([oa_p, oa_s], axis=0), o_sgu, w_out_ab[j].astype(BF16), F32)
        else:
            w_kpe = w_in_c[j][:, Q_LORA + KV_LORA:]
            n_c = Q_LORA + KV_LORA + 2 * ROPE_DIM
            w_in_ext = jnp.concatenate([w_in_c[j], _rot_cols(w_kpe), jnp.zeros((D, 2048 - n_c), F32)], axis=1)
            pc = _mm(a, w_in_ext.astype(BF16), F32)
            cq = _rmsnorm(pc[:, :Q_LORA], q_norm_g[j]).astype(BF16)
            ckv = _rmsnorm(pc[:, Q_LORA:Q_LORA + KV_LORA], kv_norm_g[j])
            kpe_raw = pc[:, Q_LORA + KV_LORA:Q_LORA + KV_LORA + ROPE_DIM]
            kpe_rot = pc[:, Q_LORA + KV_LORA + ROPE_DIM:n_c]
            kpe = (kpe_raw * cos_t + kpe_rot * sin_t).astype(BF16)
            ckv_list.append(ckv[:rows_p].reshape(n_p, s_p, KV_LORA))
            kpe_list.append(kpe_raw[:rows_p].reshape(n_p, s_p, ROPE_DIM))
            wq = w_uq[j].reshape(Q_LORA, C_HEADS, NOPE_DIM + ROPE_DIM) * q_scale
            wq_n = wq[:, :, :NOPE_DIM].reshape(Q_LORA, C_HEADS * NOPE_DIM)
            wq_p = wq[:, :, NOPE_DIM:].reshape(Q_LORA, C_HEADS * ROPE_DIM)
            n_n, n_r = C_HEADS * NOPE_DIM, C_HEADS * ROPE_DIM
            qn = _mm(cq, wq_n.astype(BF16), BF16)
            qx = _mm(cq, jnp.concatenate([wq_p, _rot_cols(wq_p)], axis=1).astype(BF16), F32)
            qp = (qx[:, :n_r].reshape(rows, C_HEADS, ROPE_DIM) * cos_t[:, None, :]
                  + qx[:, n_r:].reshape(rows, C_HEADS, ROPE_DIM) * sin_t[:, None, :]).astype(BF16).reshape(rows, n_r)
            w_ukv_b = w_ukv[j].astype(BF16)
            kv = _mm(ckv.astype(BF16), w_ukv_b, BF16)
            kvc = _mm(cache_ckv[:, j].reshape(n_s * past, KV_LORA).astype(BF16), w_ukv_b, BF16)
            kpec = cache_kpe[:, j].reshape(n_s * past, ROPE_DIM).astype(BF16)
            o_p = _attn(qn, qp, kv, kpe, 0, n_p, s_p, s_p)
            o_s = _attn(qn, qp, kv, kpe, rows_p, n_s, s_s, ATTN_TQ, cache=(kvc, kpec, past))
            m = _mm(jnp.concatenate([o_p, o_s], axis=0), w_o_c[j].astype(BF16), F32)
        h, bmod, logits_t = _ln_call("moe", h, m, g1, ln_g[layer, 0][None, :], ln_b[layer, 0][None, :],
                                     sc2, sh2, w_router_t)
        y = _moe(bmod, logits_t, b_router_t, w_gate_e, w_up_e, w_down_e, layer)
        if layer + 1 < DEPTH:
            sh1, sc1, g1, sh2n, sc2n, g2n = modulation(layer + 1)
            h, a = _ln_call("mixer", h, y, g2, ln_g[layer, 1][None, :], ln_b[layer, 1][None, :], sc1, sh1)
            sh2, sc2, g2 = sh2n, sc2n, g2n
        else:
            (h,) = _ln_call("last", h, y, g2, ln_g[layer, 1][None, :], ln_b[layer, 1][None, :])

    y_prompt = h[:rows_p].reshape(n_p, s_p, D)
    y_sample = h[rows_p:].reshape(n_s, s_s, D)
    return (y_prompt, y_sample, jnp.stack(hg_states, axis=1), jnp.stack(ckv_list, axis=1), jnp.stack(kpe_list, axis=1))
```

```python
import functools
import math

import jax
import jax.numpy as jnp
from jax import lax
from jax.experimental import pallas as pl
from jax.experimental.pallas import tpu as pltpu

F32 = jnp.float32
BF16 = jnp.bfloat16

D_MODEL = 4096
DEPTH = 4
GRID_W = 64
A_WIDTH = D_MODEL // 2
A_HEAD_DIM = 128
A_HEADS = A_WIDTH // A_HEAD_DIM
B_WIDTH = D_MODEL // 2
SGU_CHUNK = 128
B_GROUPS = 16
C_HEADS = D_MODEL // 128
Q_LORA = D_MODEL // 4
KV_LORA = 512
NOPE_DIM = 128
ROPE_DIM = 64
V_DIM = 128
ROPE_THETA = 10000.0
N_EXPERTS = 48
N_GROUPS = 8
EXPERTS_PER_GROUP = N_EXPERTS // N_GROUPS
TOP_K = 2
EXPERT_FF = D_MODEL // 4
ADA_MULT = 6
DEEPNORM_ALPHA = (2.0 * DEPTH) ** 0.25
LN_EPS = 1e-6

LANES = 128
SUBLANES = 8
COND_ROWS = 4096
HGRN_BLOCK = 16
HGRN_HEADS = 16
HGRN_ROWS = 256
LN_ROWS = 256
ROUTER_ROWS = N_GROUPS * SUBLANES
ATTN_TQ = 512
ATTN_KB = 512
MOE_TM = 512
MOE_TF = 256
MOE_TN = 1024
VMEM_LIMIT = 56 * 1024 * 1024


def _cparams(n_axes):
    return pltpu.CompilerParams(dimension_semantics=("arbitrary",) * n_axes, vmem_limit_bytes=VMEM_LIMIT)


def _split2(a):
    hi = a.astype(BF16)
    return hi, (a - hi.astype(F32)).astype(BF16)


def _split3(a):
    hi = a.astype(BF16)
    r1 = a - hi.astype(F32)
    mid = r1.astype(BF16)
    lo = (r1 - mid.astype(F32)).astype(BF16)
    return hi, mid, lo


def _sigmoid(x):
    return 1.0 / (1.0 + jnp.exp(-x))


def _mm_kernel(x_ref, w_ref, o_ref):
    o_ref[...] = jnp.dot(x_ref[...], w_ref[...], preferred_element_type=F32).astype(o_ref.dtype)


def _mm(x, w, out_dtype, tm=512, tn=1024):
    M, K = x.shape
    N = w.shape[1]
    tm, tn = min(tm, M), min(tn, N)
    assert M % tm == 0 and N % tn == 0
    return pl.pallas_call(
        _mm_kernel,
        grid=(N // tn, M // tm),
        in_specs=[pl.BlockSpec((tm, K), lambda n, m: (m, 0)),
                  pl.BlockSpec((K, tn), lambda n, m: (0, n))],
        out_specs=pl.BlockSpec((tm, tn), lambda n, m: (m, n)),
        out_shape=jax.ShapeDtypeStruct((M, N), out_dtype),
        compiler_params=_cparams(2),
        name="dense_mm",
    )(x, w)


def _mm2_kernel(x1_ref, x2_ref, w1_ref, w2_ref, o_ref):
    o_ref[...] = (jnp.dot(x1_ref[...], w1_ref[...], preferred_element_type=F32)
                  + jnp.dot(x2_ref[...], w2_ref[...], preferred_element_type=F32)).astype(o_ref.dtype)


def _mm2(x1, x2, w, out_dtype, tm=512, tn=1024):
    M, K1 = x1.shape
    K2 = x2.shape[1]
    N = w.shape[1]
    assert K1 == K2 and w.shape[0] == K1 + K2
    return pl.pallas_call(
        _mm2_kernel,
        grid=(N // tn, M // tm),
        in_specs=[pl.BlockSpec((tm, K1), lambda n, m: (m, 0)),
                  pl.BlockSpec((tm, K2), lambda n, m: (m, 0)),
                  pl.BlockSpec((K1, tn), lambda n, m: (0, n)),
                  pl.BlockSpec((K2, tn), lambda n, m: (1, n))],
        out_specs=pl.BlockSpec((tm, tn), lambda n, m: (m, n)),
        out_shape=jax.ShapeDtypeStruct((M, N), out_dtype),
        compiler_params=_cparams(2),
        name="dense_mm2",
    )(x1, x2, w, w)


def _ada_kernel(x_ref, w_ref, b_ref, o_ref):
    o_ref[...] = jnp.dot(x_ref[...], w_ref[...].astype(BF16), preferred_element_type=F32) + b_ref[...]


def _ada_mm(x, w_all, b_all, layer, tn=1024):
    R, K = x.shape
    N = w_all.shape[2]
    return pl.pallas_call(
        _ada_kernel,
        grid=(N // tn,),
        in_specs=[pl.BlockSpec((R, K), lambda n: (0, 0)),
                  pl.BlockSpec((None, K, tn), lambda n: (layer, 0, n)),
                  pl.BlockSpec((None, 1, tn), lambda n: (layer, 0, n))],
        out_specs=pl.BlockSpec((R, tn), lambda n: (0, n)),
        out_shape=jax.ShapeDtypeStruct((R, N), F32),
        compiler_params=_cparams(1),
        name="ada_mm",
    )(x, w_all, b_all)


def _ln_kernel(*refs, mode):
    if mode == "first":
        h_ref, sc_ref, sh_ref, mod_ref = refs
        hn = h_ref[...]
    else:
        h_ref, m_ref, gate_ref, lng_ref, lnb_ref = refs[:5]
        x = DEEPNORM_ALPHA * h_ref[...] + gate_ref[...] * m_ref[...]
        mu = jnp.mean(x, axis=-1, keepdims=True)
        xc = x - mu
        var = jnp.mean(xc * xc, axis=-1, keepdims=True)
        hn = xc * lax.rsqrt(var + LN_EPS) * lng_ref[...] + lnb_ref[...]
        if mode == "last":
            refs[5][...] = hn
            return
        sc_ref, sh_ref = refs[5:7]
        if mode == "moe":
            wh_ref, wm_ref, hn_ref, mod_ref, lt_ref = refs[7:]
        else:
            hn_ref, mod_ref = refs[7:]
        hn_ref[...] = hn
    mod = hn * (1.0 + sc_ref[...]) + sh_ref[...]
    mod_ref[...] = mod.astype(mod_ref.dtype)
    if mode == "moe":
        xh, xm = _split2(mod)
        nt = (((1,), (1,)), ((), ()))
        dot = functools.partial(lax.dot_general, dimension_numbers=nt, preferred_element_type=F32)
        lt_ref[...] = dot(wh_ref[...], xh) + (dot(wh_ref[...], xm) + dot(wm_ref[...], xh))


def _ln_call(mode, h, m=None, gate=None, ln_g=None, ln_b=None, sc=None, sh=None, w_router=None):
    rows, D = h.shape
    tm = LN_ROWS
    grp = lambda i: (i * tm // COND_ROWS, 0, 0)
    row_spec = pl.BlockSpec((tm, D), lambda i: (i, 0))
    cond_spec = pl.BlockSpec((None, 1, D), grp)
    vec_spec = pl.BlockSpec((1, D), lambda i: (0, 0))
    args, in_specs = [h], [row_spec]
    if mode != "first":
        args += [m, gate, ln_g, ln_b]
        in_specs += [row_spec, cond_spec, vec_spec, vec_spec]
    if mode != "last":
        args += [sc, sh]
        in_specs += [cond_spec, cond_spec]
    out_shape, out_specs = [], []
    if mode != "first":
        out_shape.append(jax.ShapeDtypeStruct((rows, D), F32))
        out_specs.append(row_spec)
    if mode != "last":
        out_shape.append(jax.ShapeDtypeStruct((rows, D), BF16))
        out_specs.append(row_spec)
    if mode == "moe":
        args += list(w_router)
        in_specs += [pl.BlockSpec((ROUTER_ROWS, D), lambda i: (0, 0))] * 2
        out_shape.append(jax.ShapeDtypeStruct((ROUTER_ROWS, rows), F32))
        out_specs.append(pl.BlockSpec((ROUTER_ROWS, tm), lambda i: (0, i)))
    return pl.pallas_call(
        functools.partial(_ln_kernel, mode=mode),
        grid=(rows // tm,),
        in_specs=in_specs,
        out_specs=out_specs,
        out_shape=out_shape,
        compiler_params=_cparams(1),
        name="ln_" + mode,
    )(*args)


def _hgrn_kernel(*refs, rev, ts, heads, has_state, emit_state):
    refs = list(refs)
    q_ref, i_ref, z_ref = refs[:3]
    pos = 3
    if rev:
        g_ref, of_ref = refs[pos:pos + 2]
        pos += 2
    tab_ref = refs[pos]
    pos += 1
    s0_ref = None
    if has_state:
        s0_ref = refs[pos]
        pos += 1
    o_ref = refs[pos]
    pos += 1
    sn_ref = None
    if emit_state:
        sn_ref = refs[pos]
        pos += 1
    st_ref = refs[pos]

    t = pl.program_id(2)
    n_t = pl.num_programs(2)
    C = HGRN_BLOCK
    n_chunks = ts // C
    dh = A_HEAD_DIM

    @pl.when(t == 0)
    def _():
        for hd in range(heads):
            if has_state:
                st_ref[hd] = s0_ref[hd].T
            else:
                st_ref[hd] = jnp.zeros((dh, dh), F32)

    row8 = lax.broadcasted_iota(jnp.int32, (SUBLANES, dh), 0)
    ti = lax.broadcasted_iota(jnp.int32, (C, C), 0)
    si = lax.broadcasted_iota(jnp.int32, (C, C), 1)
    tri = jnp.where((ti <= si) if rev else (ti >= si), 1.0, 0.0).astype(BF16)
    neg_inf = jnp.float32(-jnp.inf)
    base = 3 if rev else 0

    def head_chunk(rows, hd):
        cols = slice(hd * dh, (hd + 1) * dh)
        log_lb = tab_ref[base:base + 1, cols]
        log_1mlb = tab_ref[base + 1:base + 2, cols]
        one_mlb = tab_ref[base + 2:base + 3, cols]
        qr = q_ref[rows, cols]
        q = qr * _sigmoid(qr)
        v = i_ref[rows, cols]
        z = z_ref[rows, cols]
        e = jnp.exp(-jnp.abs(z))
        r = 1.0 / (1.0 + e)
        log_sig = jnp.minimum(z, 0.0) + jnp.log(r)
        k = one_mlb * (jnp.where(z >= 0.0, e, 1.0) * r)
        cc = log_1mlb + log_sig
        logf = jnp.maximum(log_lb, cc) + jnp.log1p(jnp.exp(-jnp.abs(log_lb - cc)))
        hi, mid, lo = _split3(logf)
        pieces = jnp.dot(tri, jnp.concatenate([hi, mid, lo], axis=1), preferred_element_type=F32)
        b = (pieces[:, :dh] + pieces[:, dh:2 * dh]) + pieces[:, 2 * dh:]
        b_end = b[0:1, :] if rev else b[C - 1:C, :]
        st = st_ref[hd]
        qt = (q * jnp.exp(b)).astype(BF16)
        o = lax.dot_general(qt, st.astype(BF16), (((1,), (1,)), ((), ())), preferred_element_type=F32)
        halves = [o[0:SUBLANES, :], o[SUBLANES:C, :]]
        qh = [q[0:SUBLANES, :], q[SUBLANES:C, :]]
        bh = [b[0:SUBLANES, :], b[SUBLANES:C, :]]
        for s in range(C):
            bs, ks, vs = b[s:s + 1, :], k[s:s + 1, :], v[s:s + 1, :]
            for h in range(2):
                lo_row, hi_row = h * SUBLANES, h * SUBLANES + SUBLANES - 1
                if rev:
                    if lo_row > s:
                        continue
                    full = hi_row <= s
                else:
                    if hi_row < s:
                        continue
                    full = lo_row >= s
                d = bh[h] - bs
                if not full:
                    keep = (row8 + lo_row <= s) if rev else (row8 + lo_row >= s)
                    d = jnp.where(keep, d, neg_inf)
                p = (qh[h] * ks) * jnp.exp(d)
                halves[h] = halves[h] + jnp.sum(p, axis=-1, keepdims=True) * vs
        o = jnp.concatenate(halves, axis=0)
        kt = (k * jnp.exp(b_end - b)).astype(BF16)
        upd = lax.dot_general(v.astype(BF16), kt, (((0,), (0,)), ((), ())), preferred_element_type=F32)
        st_ref[hd] = st * jnp.exp(b_end) + upd
        if rev:
            tot = of_ref[rows, cols] + o
            nrm = tot * lax.rsqrt(jnp.mean(tot * tot, axis=-1, keepdims=True) + LN_EPS) * tab_ref[6:7, cols]
            gr = g_ref[rows, cols]
            o_ref[rows, cols] = (nrm * (gr * _sigmoid(gr))).astype(o_ref.dtype)
        else:
            o_ref[rows, cols] = o

    def body(ci, carry):
        c = (n_chunks - 1 - ci) if rev else ci
        rows = pl.ds(pl.multiple_of(c * C, C), C)
        for hd in range(heads):
            head_chunk(rows, hd)
        return carry

    lax.fori_loop(0, n_chunks, body, 0)

    if emit_state:
        @pl.when(t == n_t - 1)
        def _():
            for hd in range(heads):
                sn_ref[hd] = st_ref[hd].T


def _hgrn_dir(proj, part, tab, state, layer_j, row0, n_seq, seq, rev, emit_state):
    has_state = state is not None
    G = HGRN_HEADS
    ts = min(HGRN_ROWS, seq)
    n_t = seq // ts
    hg_n = A_HEADS // G
    blk0 = row0 // ts
    tt = (lambda t: n_t - 1 - t) if rev else (lambda t: t)
    col = lambda k: (lambda b, hg, t: (blk0 + b * n_t + tt(t), k * hg_n + hg))
    in_specs = [pl.BlockSpec((ts, G * A_HEAD_DIM), col(k)) for k in (0, 1, 3 if rev else 2)]
    args = [proj, proj, proj]
    if rev:
        in_specs += [pl.BlockSpec((ts, G * A_HEAD_DIM), col(4)),
                     pl.BlockSpec((ts, G * A_HEAD_DIM), lambda b, hg, t: (b * n_t + tt(t), hg))]
        args += [proj, part]
    in_specs.append(pl.BlockSpec((SUBLANES, G * A_HEAD_DIM), lambda b, hg, t: (0, hg)))
    args.append(tab)
    if has_state:
        in_specs.append(pl.BlockSpec((None, None, None, G, A_HEAD_DIM, A_HEAD_DIM),
                                     lambda b, hg, t: (b, layer_j, 1 if rev else 0, hg, 0, 0)))
        args.append(state)
    out_specs = [pl.BlockSpec((ts, G * A_HEAD_DIM), lambda b, hg, t: (b * n_t + tt(t), hg))]
    out_shape = [jax.ShapeDtypeStruct((n_seq * seq, A_WIDTH), BF16 if rev else F32)]
    if emit_state:
        out_specs.append(pl.BlockSpec((None, G, A_HEAD_DIM, A_HEAD_DIM), lambda b, hg, t: (b, hg, 0, 0)))
        out_shape.append(jax.ShapeDtypeStruct((n_seq, A_HEADS, A_HEAD_DIM, A_HEAD_DIM), F32))
    return pl.pallas_call(
        functools.partial(_hgrn_kernel, rev=rev, ts=ts, heads=G, has_state=has_state, emit_state=emit_state),
        grid=(n_seq, hg_n, n_t),
        in_specs=in_specs,
        out_specs=out_specs,
        out_shape=out_shape,
        scratch_shapes=[pltpu.VMEM((G, A_HEAD_DIM, A_HEAD_DIM), F32)],
        compiler_params=_cparams(3),
        name="hgrn_bwd" if rev else "hgrn_fwd",
    )(*args)


def _hgrn(proj, tab, state, layer_j, row0, n_seq, seq, emit_state):
    fwd = _hgrn_dir(proj, None, tab, state, layer_j, row0, n_seq, seq, False, emit_state)
    bwd = _hgrn_dir(proj, fwd[0], tab, state, layer_j, row0, n_seq, seq, True, emit_state)
    if emit_state:
        return bwd[0], jnp.stack([fwd[1], bwd[1]], axis=1)
    return bwd[0], None


def _gelu_tanh(x):
    return 0.5 * x * (1.0 + jnp.tanh(math.sqrt(2.0 / math.pi) * (x + 0.044715 * (x * x * x))))


def _sgu_kernel(u_ref, v_ref, lng_ref, lnb_ref, w_ref, bias_ref, o_ref):
    v = _gelu_tanh(v_ref[...])
    mu = jnp.mean(v, axis=-1, keepdims=True)
    vc = v - mu
    var = jnp.mean(vc * vc, axis=-1, keepdims=True)
    vn = (vc * lax.rsqrt(var + LN_EPS) * lng_ref[...] + lnb_ref[...]).astype(BF16)
    for g in range(B_GROUPS):
        sl = slice(g * SGU_CHUNK, (g + 1) * SGU_CHUNK)
        mixed = jnp.dot(w_ref[g], vn[:, sl], preferred_element_type=F32) + bias_ref[:, sl]
        o_ref[:, sl] = (_gelu_tanh(u_ref[:, sl]) * mixed).astype(o_ref.dtype)


def _sgu(proj, ln_g, ln_b, w_s, bias_full):
    rows = proj.shape[0]
    ublk = 5 * A_WIDTH // B_WIDTH
    return pl.pallas_call(
        _sgu_kernel,
        grid=(rows // SGU_CHUNK,),
        in_specs=[pl.BlockSpec((SGU_CHUNK, B_WIDTH), lambda r: (r, ublk)),
                  pl.BlockSpec((SGU_CHUNK, B_WIDTH), lambda r: (r, ublk + 1)),
                  pl.BlockSpec((1, B_WIDTH), lambda r: (0, 0)),
                  pl.BlockSpec((1, B_WIDTH), lambda r: (0, 0)),
                  pl.BlockSpec((B_GROUPS, SGU_CHUNK, SGU_CHUNK), lambda r: (0, 0, 0)),
                  pl.BlockSpec((SGU_CHUNK, B_WIDTH), lambda r: (0, 0))],
        out_specs=pl.BlockSpec((SGU_CHUNK, B_WIDTH), lambda r: (r, 0)),
        out_shape=jax.ShapeDtypeStruct((rows, B_WIDTH), BF16),
        compiler_params=_cparams(1),
        name="sgu",
    )(proj, proj, ln_g, ln_b, w_s, bias_full)


def _attn_kernel(*refs, with_cache, kb):
    if with_cache:
        (q_ref, cos_ref, sin_ref, kv_ref, kpe_ref, kvc_ref, kpec_ref, o_ref,
         kx_ref, vx_ref, s_ref, kxc_ref, vxc_ref, sc_ref) = refs
    else:
        q_ref, cos_ref, sin_ref, kv_ref, kpe_ref, o_ref, kx_ref, vx_ref, s_ref = refs
    nt = (((1,), (1,)), ((), ()))
    pair = NOPE_DIM + V_DIM
    qw = NOPE_DIM + 2 * ROPE_DIM
    tq = q_ref.shape[0]
    seq = kv_ref.shape[0]

    @pl.when(pl.program_id(2) == 0)
    def _():
        pad = qw - NOPE_DIM - ROPE_DIM
        for i in range(2):
            kx_ref[i, :, 0:NOPE_DIM] = kv_ref[:, i * pair:i * pair + NOPE_DIM]
            kx_ref[i, :, NOPE_DIM:NOPE_DIM + ROPE_DIM] = kpe_ref[...]
            kx_ref[i, :, NOPE_DIM + ROPE_DIM:qw] = jnp.zeros((seq, pad), BF16)
            vx_ref[i, :, 0:V_DIM] = kv_ref[:, i * pair + NOPE_DIM:(i + 1) * pair]
            vx_ref[i, :, V_DIM:2 * V_DIM] = jnp.ones((seq, V_DIM), BF16)
            if with_cache:
                past = kvc_ref.shape[0]
                kxc_ref[i, :, 0:NOPE_DIM] = kvc_ref[:, i * pair:i * pair + NOPE_DIM]
                kxc_ref[i, :, NOPE_DIM:NOPE_DIM + ROPE_DIM] = kpec_ref[...]
                kxc_ref[i, :, NOPE_DIM + ROPE_DIM:qw] = jnp.zeros((past, pad), BF16)
                vxc_ref[i, :, 0:V_DIM] = kvc_ref[:, i * pair + NOPE_DIM:(i + 1) * pair]
                vxc_ref[i, :, V_DIM:2 * V_DIM] = jnp.ones((past, V_DIM), BF16)

    cos, sin = cos_ref[...], sin_ref[...]
    n_kb = seq // kb
    for i in range(2):
        q0 = i * qw
        qpe = (q_ref[:, q0 + NOPE_DIM:q0 + NOPE_DIM + ROPE_DIM].astype(F32) * cos
               + q_ref[:, q0 + NOPE_DIM + ROPE_DIM:q0 + qw].astype(F32) * sin).astype(BF16)
        q = jnp.concatenate([q_ref[:, q0:q0 + NOPE_DIM], qpe, jnp.zeros((tq, qw - NOPE_DIM - ROPE_DIM), BF16)], axis=1)
        m = None
        for j in range(n_kb):
            s = lax.dot_general(q, kx_ref[i, j * kb:(j + 1) * kb, :], nt, preferred_element_type=F32)
            s_ref[i, :, j * kb:(j + 1) * kb] = s
            bm = jnp.max(s, axis=-1, keepdims=True)
            m = bm if m is None else jnp.maximum(m, bm)
        if with_cache:
            s = lax.dot_general(q, kxc_ref[i], nt, preferred_element_type=F32)
            sc_ref[i] = s
            m = jnp.maximum(m, jnp.max(s, axis=-1, keepdims=True))
        acc = jnp.zeros((tq, 2 * V_DIM), F32)
        for j in range(n_kb):
            p = jnp.exp2(s_ref[i, :, j * kb:(j + 1) * kb] - m).astype(BF16)
            acc = acc + jnp.dot(p, vx_ref[i, j * kb:(j + 1) * kb, :], preferred_element_type=F32)
        if with_cache:
            acc = acc + jnp.dot(jnp.exp2(sc_ref[i] - m).astype(BF16), vxc_ref[i], preferred_element_type=F32)
        o_ref[:, i * V_DIM:(i + 1) * V_DIM] = (acc[:, 0:V_DIM] / acc[:, V_DIM:V_DIM + 1]).astype(o_ref.dtype)


def _attn(q, cos_t, sin_t, kv, kpe, row0, n_seq, seq, tq, cache=None):
    nq = seq // tq
    qb0, kb0 = row0 // tq, row0 // seq
    pair = 2 * (NOPE_DIM + V_DIM)
    qw = NOPE_DIM + 2 * ROPE_DIM
    kb = min(ATTN_KB, seq)
    qrow = lambda b, h, i: (qb0 + b * nq + i, 0)
    in_specs = [pl.BlockSpec((tq, 2 * qw), lambda b, h, i: (qb0 + b * nq + i, h)),
                pl.BlockSpec((tq, ROPE_DIM), qrow),
                pl.BlockSpec((tq, ROPE_DIM), qrow),
                pl.BlockSpec((seq, pair), lambda b, h, i: (kb0 + b, h)),
                pl.BlockSpec((seq, ROPE_DIM), lambda b, h, i: (kb0 + b, 0))]
    args = [q, cos_t, sin_t, kv, kpe]
    scratch = [pltpu.VMEM((2, seq, qw), BF16), pltpu.VMEM((2, seq, 2 * V_DIM), BF16), pltpu.VMEM((2, tq, seq), F32)]
    if cache is not None:
        kvc, kpec, past = cache
        in_specs += [pl.BlockSpec((past, pair), lambda b, h, i: (b, h)),
                     pl.BlockSpec((past, ROPE_DIM), lambda b, h, i: (b, 0))]
        args += [kvc, kpec]
        scratch += [pltpu.VMEM((2, past, qw), BF16), pltpu.VMEM((2, past, 2 * V_DIM), BF16),
                    pltpu.VMEM((2, tq, past), F32)]
    return pl.pallas_call(
        functools.partial(_attn_kernel, with_cache=cache is not None, kb=kb),
        grid=(n_seq, C_HEADS // 2, nq),
        in_specs=in_specs,
        out_specs=pl.BlockSpec((tq, 2 * V_DIM), lambda b, h, i: (b * nq + i, h)),
        out_shape=jax.ShapeDtypeStruct((n_seq * seq, C_HEADS * V_DIM), BF16),
        scratch_shapes=scratch,
        compiler_params=_cparams(3),
        name="mla_attn",
    )(*args)


def _moe_up_kernel(be_ref, nu_ref, x_ref, wg_ref, wu_ref, h_ref, wgb_ref, wub_ref):
    i = pl.program_id(1)
    fresh = jnp.logical_or(i == 0, be_ref[i] != be_ref[jnp.maximum(i - 1, 0)])

    @pl.when(fresh)
    def _():
        wgb_ref[...] = wg_ref[...].astype(BF16)
        wub_ref[...] = wu_ref[...].astype(BF16)

    @pl.when(i < nu_ref[0])
    def _():
        x = x_ref[...]
        a = jnp.dot(x, wgb_ref[...], preferred_element_type=F32)
        b = jnp.dot(x, wub_ref[...], preferred_element_type=F32)
        h_ref[...] = (a * _sigmoid(a) * b).astype(h_ref.dtype)

    @pl.when(i >= nu_ref[0])
    def _():
        h_ref[...] = jnp.zeros_like(h_ref)


def _moe_down_kernel(be_ref, nu_ref, h_ref, wd_ref, y_ref, wdb_ref):
    i = pl.program_id(1)
    fresh = jnp.logical_or(i == 0, be_ref[i] != be_ref[jnp.maximum(i - 1, 0)])

    @pl.when(fresh)
    def _():
        wdb_ref[...] = wd_ref[...].astype(BF16)

    @pl.when(i < nu_ref[0])
    def _():
        y_ref[...] = jnp.dot(h_ref[...], wdb_ref[...], preferred_element_type=F32)

    @pl.when(i >= nu_ref[0])
    def _():
        y_ref[...] = jnp.zeros_like(y_ref)


def _moe_ffn(xp, blk_expert, n_used, w_gate, w_up, w_down, layer):
    n_rows, D = xp.shape
    FF = w_gate.shape[-1]
    n_blk = n_rows // MOE_TM
    h = pl.pallas_call(
        _moe_up_kernel,
        grid_spec=pltpu.PrefetchScalarGridSpec(
            num_scalar_prefetch=2,
            grid=(FF // MOE_TF, n_blk),
            in_specs=[pl.BlockSpec((MOE_TM, D), lambda f, i, be, nu: (i, 0)),
                      pl.BlockSpec((None, None, D, MOE_TF), lambda f, i, be, nu: (layer, be[i], 0, f)),
                      pl.BlockSpec((None, None, D, MOE_TF), lambda f, i, be, nu: (layer, be[i], 0, f))],
            out_specs=pl.BlockSpec((MOE_TM, MOE_TF), lambda f, i, be, nu: (i, f)),
            scratch_shapes=[pltpu.VMEM((D, MOE_TF), BF16), pltpu.VMEM((D, MOE_TF), BF16)]),
        out_shape=jax.ShapeDtypeStruct((n_rows, FF), BF16),
        compiler_params=_cparams(2),
        name="moe_up",
    )(blk_expert, n_used, xp, w_gate, w_up)
    return pl.pallas_call(
        _moe_down_kernel,
        grid_spec=pltpu.PrefetchScalarGridSpec(
            num_scalar_prefetch=2,
            grid=(D // MOE_TN, n_blk),
            in_specs=[pl.BlockSpec((MOE_TM, FF), lambda n, i, be, nu: (i, 0)),
                      pl.BlockSpec((None, None, FF, MOE_TN), lambda n, i, be, nu: (layer, be[i], 0, n))],
            out_specs=pl.BlockSpec((MOE_TM, MOE_TN), lambda n, i, be, nu: (i, n)),
            scratch_shapes=[pltpu.VMEM((FF, MOE_TN), BF16)]),
        out_shape=jax.ShapeDtypeStruct((n_rows, D), F32),
        compiler_params=_cparams(2),
        name="moe_down",
    )(blk_expert, n_used, h, w_down)


def _route(logits_t, b_router_t):
    n_tok = logits_t.shape[1]
    scores = jax.nn.sigmoid(logits_t).reshape(N_GROUPS, SUBLANES, n_tok)
    r_io = lax.broadcasted_iota(jnp.int32, (N_GROUPS, SUBLANES, n_tok), 1)
    g_io = lax.broadcasted_iota(jnp.int32, (N_GROUPS, n_tok), 0)
    sel = jnp.where(r_io < EXPERTS_PER_GROUP, scores + b_router_t.reshape(N_GROUPS, SUBLANES, 1), -jnp.inf)
    m1 = jnp.max(sel, axis=1)
    i1 = jnp.min(jnp.where(sel == m1[:, None, :], r_io, SUBLANES), axis=1)
    sel2 = jnp.where(r_io == i1[:, None, :], -jnp.inf, sel)
    m2 = jnp.max(sel2, axis=1)
    i2 = jnp.min(jnp.where(sel2 == m2[:, None, :], r_io, SUBLANES), axis=1)
    group_score = m1 + m2
    g_idx = jnp.min(jnp.where(group_score == jnp.max(group_score, axis=0, keepdims=True), g_io, N_GROUPS), axis=0)
    best = g_io == g_idx[None, :]
    l1 = jnp.sum(jnp.where(best, i1, 0), axis=0)
    l2 = jnp.sum(jnp.where(best, i2, 0), axis=0)
    local = jnp.stack([l1, l2], axis=0)
    pick = best[None, :, None, :] & (r_io[None] == local[:, None, None, :])
    s = jnp.sum(jnp.where(pick, scores[None], 0.0), axis=(1, 2))
    gate = s / jnp.sum(s, axis=0, keepdims=True)
    return (g_idx[None, :] * EXPERTS_PER_GROUP + local).astype(jnp.int32), gate


def _dispatch(expert_idx):
    n_tok = expert_idx.shape[1]
    n_flat = TOP_K * n_tok
    nb = n_flat // LANES
    e_flat = expert_idx.reshape(nb, LANES)
    onehot = e_flat[None] == lax.broadcasted_iota(jnp.int32, (N_EXPERTS, nb, LANES), 0)
    upper = (lax.broadcasted_iota(jnp.int32, (LANES, LANES), 0)
             <= lax.broadcasted_iota(jnp.int32, (LANES, LANES), 1)).astype(BF16)
    within = jnp.einsum('ebl,lm->ebm', onehot.astype(BF16), upper, preferred_element_type=F32)
    blk_cnt = within[:, :, LANES - 1]
    blk_end = jnp.cumsum(blk_cnt, axis=1)
    counts = blk_end[:, -1].astype(jnp.int32)
    padded = (counts + MOE_TM - 1) // MOE_TM * MOE_TM
    pad_end = jnp.cumsum(padded)
    pad_start = pad_end - padded
    offs = (blk_end - blk_cnt)[:, :, None] + pad_start.astype(F32)[:, None, None] + within - 1.0
    pos = jnp.sum(jnp.where(onehot, offs, 0.0), axis=0).astype(jnp.int32).reshape(n_flat)
    n_blk = (n_flat + N_EXPERTS * (MOE_TM - 1) + MOE_TM - 1) // MOE_TM
    blk_expert = jnp.minimum(jnp.searchsorted(pad_end, jnp.arange(n_blk, dtype=jnp.int32) * MOE_TM, side='right'),
                             N_EXPERTS - 1).astype(jnp.int32)
    n_used = (pad_end[-1:] // MOE_TM).astype(jnp.int32)
    tok = lax.rem(jnp.arange(n_flat, dtype=jnp.int32), n_tok)
    src = jnp.zeros((n_blk * MOE_TM,), jnp.int32).at[pos].set(tok, unique_indices=True)
    return pos.reshape(TOP_K, n_tok), blk_expert, n_used, src


def _moe(xb, logits_t, b_router_t, w_gate, w_up, w_down, layer):
    expert_idx, gate = _route(logits_t, b_router_t)
    pos, blk_expert, n_used, src = _dispatch(expert_idx)
    xp = jnp.take(xb, src, axis=0)
    yp = _moe_ffn(xp, blk_expert, n_used, w_gate, w_up, w_down, layer)
    return yp[pos[0]] * gate[0][:, None] + yp[pos[1]] * gate[1][:, None]


def _rmsnorm(x, g):
    return x * lax.rsqrt(jnp.mean(x * x, axis=-1, keepdims=True) + LN_EPS) * g


def _rot_cols(w):
    shp = w.shape
    w4 = w.reshape(shp[:-1] + (shp[-1] // 32, 2, 16))
    return jnp.stack([-w4[..., 1, :], w4[..., 0, :]], axis=-2).reshape(shp)


def _rope_tables(n_prompt_rows, n_seq, seq):
    t = jnp.arange(seq)
    row = (t // GRID_W).astype(F32)
    colp = (t % GRID_W).astype(F32)
    n_freq = ROPE_DIM // 4
    inv = ROPE_THETA ** (-jnp.arange(n_freq, dtype=F32) / n_freq)
    ang = jnp.stack([row[:, None] * inv, colp[:, None] * inv], axis=1)
    cos = jnp.broadcast_to(jnp.cos(ang)[:, :, None, :], (seq, 2, 2, n_freq)).reshape(seq, ROPE_DIM)
    sin = jnp.broadcast_to(jnp.sin(ang)[:, :, None, :], (seq, 2, 2, n_freq)).reshape(seq, ROPE_DIM)
    cos = jnp.concatenate([jnp.ones((n_prompt_rows, ROPE_DIM), F32), jnp.tile(cos, (n_seq, 1))], axis=0)
    sin = jnp.concatenate([jnp.zeros((n_prompt_rows, ROPE_DIM), F32), jnp.tile(sin, (n_seq, 1))], axis=0)
    return cos, sin


def _router_weights(w_router, b_router):
    D = w_router.shape[0]
    wt = w_router.T.reshape(N_GROUPS, EXPERTS_PER_GROUP, D)
    wt = jnp.pad(wt, ((0, 0), (0, SUBLANES - EXPERTS_PER_GROUP), (0, 0))).reshape(ROUTER_ROWS, D)
    bt = jnp.pad(b_router.astype(F32).reshape(N_GROUPS, EXPERTS_PER_GROUP),
                 ((0, 0), (0, SUBLANES - EXPERTS_PER_GROUP))).reshape(ROUTER_ROWS)
    return _split2(wt.astype(F32)), bt


def kernel(x_prompt, x_sample, state_hgrn, cache_ckv, cache_kpe, c, c_ctx, w_ada, b_ada, ln_g, ln_b, w_in_ab, hgrn_lb, hgrn_norm_g, sgu_ln_g, sgu_ln_b, sgu_w, sgu_b, w_out_ab, w_in_c, q_norm_g, kv_norm_g, w_uq, w_ukv, w_o_c, w_router, b_router, w_gate_e, w_up_e, w_down_e):
    n_p, s_p, D = x_prompt.shape
    n_s, s_s, _ = x_sample.shape
    past = cache_ckv.shape[2]
    rows_p, rows_s = n_p * s_p, n_s * s_s
    rows = rows_p + rows_s
    assert rows_p % COND_ROWS == 0 and s_s == COND_ROWS
    n_grp = rows // COND_ROWS
    grp_p = rows_p // COND_ROWS

    h = jnp.concatenate([x_prompt.reshape(rows_p, D), x_sample.reshape(rows_s, D)], axis=0)

    cond = jnp.concatenate([jnp.broadcast_to(c_ctx[None, :], (grp_p, D)), c], axis=0)
    cond_rows = -(-n_grp // 16) * 16
    cond = jnp.pad(jax.nn.silu(cond.astype(F32)), ((0, cond_rows - n_grp), (0, 0))).astype(BF16)
    b_ada3 = b_ada.reshape(DEPTH, 1, ADA_MULT * D)

    def modulation(layer):
        mod = _ada_mm(cond, w_ada, b_ada3, layer)[:n_grp]
        return [m[:, None, :] for m in jnp.split(mod, ADA_MULT, axis=-1)]

    p = jax.nn.softmax(hgrn_lb.astype(F32), axis=0)
    lbs = jnp.cumsum(p, axis=0) - p[0:1]

    cos_t, sin_t = _rope_tables(rows_p, n_s, s_s)
    w_router_t, b_router_t = _router_weights(w_router, b_router)
    q_scale = math.log2(math.e) / math.sqrt(NOPE_DIM + ROPE_DIM)

    hg_states, ckv_list, kpe_list = [], [], []
    sh1, sc1, g1, sh2, sc2, g2 = modulation(0)
    (a,) = _ln_call("first", h, sc=sc1, sh=sh1)
    for layer in range(DEPTH):
        j = layer // 2
        if layer % 2 == 0:
            proj = _mm(a, w_in_ab[j].astype(BF16), F32)
            lb = lbs[j]
            tab = jnp.concatenate([jnp.log(lb[0:1]), jnp.log1p(-lb[0:1]), 1.0 - lb[0:1],
                                   jnp.log(lb[1:2]), jnp.log1p(-lb[1:2]), 1.0 - lb[1:2],
                                   hgrn_norm_g[j][None, :], jnp.zeros((1, A_WIDTH), F32)], axis=0)
            oa_p, st_p = _hgrn(proj, tab, None, j, 0, n_p, s_p, True)
            oa_s, _ = _hgrn(proj, tab, state_hgrn, j, rows_p, n_s, s_s, False)
            hg_states.append(st_p)
            bias_full = jnp.repeat(sgu_b[j].T, SGU_CHUNK, axis=1)
            o_sgu = _sgu(proj, sgu_ln_g[j][None, :], sgu_ln_b[j][None, :], sgu_w[j].astype(BF16), bias_full)
            m = _mm2(jnp.concatenate([oa_p, oa_s], axis=0), o_sgu, w_out_ab[j].astype(BF16), F32)
        else:
            w_kpe = w_in_c[j][:, Q_LORA + KV_LORA:]
            n_c = Q_LORA + KV_LORA + 2 * ROPE_DIM
            w_in_ext = jnp.concatenate([w_in_c[j], _rot_cols(w_kpe), jnp.zeros((D, 2048 - n_c), F32)], axis=1)
            pc = _mm(a, w_in_ext.astype(BF16), F32)
            cq = _rmsnorm(pc[:, :Q_LORA], q_norm_g[j]).astype(BF16)
            ckv = _rmsnorm(pc[:, Q_LORA:Q_LORA + KV_LORA], kv_norm_g[j])
            kpe_raw = pc[:, Q_LORA + KV_LORA:Q_LORA + KV_LORA + ROPE_DIM]
            kpe_rot = pc[:, Q_LORA + KV_LORA + ROPE_DIM:n_c]
            kpe = (kpe_raw * cos_t + kpe_rot * sin_t).astype(BF16)
            ckv_list.append(ckv[:rows_p].reshape(n_p, s_p, KV_LORA))
            kpe_list.append(kpe_raw[:rows_p].reshape(n_p, s_p, ROPE_DIM))
            wq = w_uq[j].reshape(Q_LORA, C_HEADS, NOPE_DIM + ROPE_DIM) * q_scale
            wq_p = wq[:, :, NOPE_DIM:]
            wq_ext = jnp.concatenate([wq[:, :, :NOPE_DIM], wq_p, _rot_cols(wq_p)], axis=2)
            q = _mm(cq, wq_ext.reshape(Q_LORA, C_HEADS * (NOPE_DIM + 2 * ROPE_DIM)).astype(BF16), BF16)
            w_ukv_b = w_ukv[j].astype(BF16)
            kv = _mm(ckv.astype(BF16), w_ukv_b, BF16)
            kvc = _mm(cache_ckv[:, j].reshape(n_s * past, KV_LORA).astype(BF16), w_ukv_b, BF16)
            kpec = cache_kpe[:, j].reshape(n_s * past, ROPE_DIM).astype(BF16)
            o_p = _attn(q, cos_t, sin_t, kv, kpe, 0, n_p, s_p, s_p)
            o_s = _attn(q, cos_t, sin_t, kv, kpe, rows_p, n_s, s_s, ATTN_TQ, cache=(kvc, kpec, past))
            m = _mm(jnp.concatenate([o_p, o_s], axis=0), w_o_c[j].astype(BF16), F32)
        h, bmod, logits_t = _ln_call("moe", h, m, g1, ln_g[layer, 0][None, :], ln_b[layer, 0][None, :],
                                     sc2, sh2, w_router_t)
        y = _moe(bmod, logits_t, b_router_t, w_gate_e, w_up_e, w_down_e, layer)
        if layer + 1 < DEPTH:
            sh1, sc1, g1, sh2n, sc2n, g2n = modulation(layer + 1)
            h, a = _ln_call("mixer", h, y, g2, ln_g[layer, 1][None, :], ln_b[layer, 1][None, :], sc1, sh1)
            sh2, sc2, g2 = sh2n, sc2n, g2n
        else:
            (h,) = _ln_call("last", h, y, g2, ln_g[layer, 1][None, :], ln_b[layer, 1][None, :])

    y_prompt = h[:rows_p].reshape(n_p, s_p, D)
    y_sample = h[rows_p:].reshape(n_s, s_s, D)
    return (y_prompt, y_sample, jnp.stack(hg_states, axis=1), jnp.stack(ckv_list, axis=1), jnp.stack(kpe_list, axis=1))
```

```python
import functools
import math

import jax
import jax.numpy as jnp
from jax import lax
from jax.experimental import pallas as pl
from jax.experimental.pallas import tpu as pltpu

F32 = jnp.float32
BF16 = jnp.bfloat16

D_MODEL = 4096
DEPTH = 4
GRID_W = 64
A_WIDTH = D_MODEL // 2
A_HEAD_DIM = 128
A_HEADS = A_WIDTH // A_HEAD_DIM
B_WIDTH = D_MODEL // 2
SGU_CHUNK = 128
B_GROUPS = 16
C_HEADS = D_MODEL // 128
Q_LORA = D_MODEL // 4
KV_LORA = 512
NOPE_DIM = 128
ROPE_DIM = 64
V_DIM = 128
ROPE_THETA = 10000.0
N_EXPERTS = 48
N_GROUPS = 8
EXPERTS_PER_GROUP = N_EXPERTS // N_GROUPS
TOP_K = 2
EXPERT_FF = D_MODEL // 4
ADA_MULT = 6
DEEPNORM_ALPHA = (2.0 * DEPTH) ** 0.25
LN_EPS = 1e-6

LANES = 128
SUBLANES = 8
COND_ROWS = 4096
HGRN_BLOCK = 16
HGRN_HEADS = 16
HGRN_ROWS = 256
LN_ROWS = 256
ROUTER_ROWS = N_GROUPS * SUBLANES
ATTN_TQ = 512
ATTN_KB = 512
MOE_TM = 512
MOE_TF = 256
MOE_TN = 1024
VMEM_LIMIT = 56 * 1024 * 1024


def _cparams(n_axes):
    return pltpu.CompilerParams(dimension_semantics=("arbitrary",) * n_axes, vmem_limit_bytes=VMEM_LIMIT)


def _split2(a):
    hi = a.astype(BF16)
    return hi, (a - hi.astype(F32)).astype(BF16)


def _split3(a):
    hi = a.astype(BF16)
    r1 = a - hi.astype(F32)
    mid = r1.astype(BF16)
    lo = (r1 - mid.astype(F32)).astype(BF16)
    return hi, mid, lo


def _sigmoid(x):
    return 1.0 / (1.0 + jnp.exp(-x))


def _mm_kernel(x_ref, w_ref, o_ref):
    o_ref[...] = jnp.dot(x_ref[...], w_ref[...], preferred_element_type=F32).astype(o_ref.dtype)


def _mm(x, w, out_dtype, tm=512, tn=1024):
    M, K = x.shape
    N = w.shape[1]
    tm, tn = min(tm, M), min(tn, N)
    assert M % tm == 0 and N % tn == 0
    return pl.pallas_call(
        _mm_kernel,
        grid=(N // tn, M // tm),
        in_specs=[pl.BlockSpec((tm, K), lambda n, m: (m, 0)),
                  pl.BlockSpec((K, tn), lambda n, m: (0, n))],
        out_specs=pl.BlockSpec((tm, tn), lambda n, m: (m, n)),
        out_shape=jax.ShapeDtypeStruct((M, N), out_dtype),
        compiler_params=_cparams(2),
        name="dense_mm",
    )(x, w)


def _mm2_kernel(x1_ref, x2_ref, w1_ref, w2_ref, o_ref):
    o_ref[...] = (jnp.dot(x1_ref[...], w1_ref[...], preferred_element_type=F32)
                  + jnp.dot(x2_ref[...], w2_ref[...], preferred_element_type=F32)).astype(o_ref.dtype)


def _mm2(x1, x2, w, out_dtype, tm=512, tn=1024):
    M, K1 = x1.shape
    K2 = x2.shape[1]
    N = w.shape[1]
    assert K1 == K2 and w.shape[0] == K1 + K2
    return pl.pallas_call(
        _mm2_kernel,
        grid=(N // tn, M // tm),
        in_specs=[pl.BlockSpec((tm, K1), lambda n, m: (m, 0)),
                  pl.BlockSpec((tm, K2), lambda n, m: (m, 0)),
                  pl.BlockSpec((K1, tn), lambda n, m: (0, n)),
                  pl.BlockSpec((K2, tn), lambda n, m: (1, n))],
        out_specs=pl.BlockSpec((tm, tn), lambda n, m: (m, n)),
        out_shape=jax.ShapeDtypeStruct((M, N), out_dtype),
        compiler_params=_cparams(2),
        name="dense_mm2",
    )(x1, x2, w, w)


def _ada_kernel(x_ref, w_ref, b_ref, o_ref):
    o_ref[...] = jnp.dot(x_ref[...], w_ref[...].astype(BF16), preferred_element_type=F32) + b_ref[...]


def _ada_mm(x, w_all, b_all, layer, tn=1024):
    R, K = x.shape
    N = w_all.shape[2]
    return pl.pallas_call(
        _ada_kernel,
        grid=(N // tn,),
        in_specs=[pl.BlockSpec((R, K), lambda n: (0, 0)),
                  pl.BlockSpec((None, K, tn), lambda n: (layer, 0, n)),
                  pl.BlockSpec((None, 1, tn), lambda n: (layer, 0, n))],
        out_specs=pl.BlockSpec((R, tn), lambda n: (0, n)),
        out_shape=jax.ShapeDtypeStruct((R, N), F32),
        compiler_params=_cparams(1),
        name="ada_mm",
    )(x, w_all, b_all)


def _ln_kernel(*refs, mode):
    if mode == "first":
        h_ref, sc_ref, sh_ref, mod_ref = refs
        hn = h_ref[...]
    else:
        h_ref, m_ref, gate_ref, lng_ref, lnb_ref = refs[:5]
        x = DEEPNORM_ALPHA * h_ref[...] + gate_ref[...] * m_ref[...]
        mu = jnp.mean(x, axis=-1, keepdims=True)
        xc = x - mu
        var = jnp.mean(xc * xc, axis=-1, keepdims=True)
        hn = xc * lax.rsqrt(var + LN_EPS) * lng_ref[...] + lnb_ref[...]
        if mode == "last":
            refs[5][...] = hn
            return
        sc_ref, sh_ref = refs[5:7]
        if mode == "moe":
            wh_ref, wm_ref, hn_ref, mod_ref, lt_ref = refs[7:]
        else:
            hn_ref, mod_ref = refs[7:]
        hn_ref[...] = hn
    mod = hn * (1.0 + sc_ref[...]) + sh_ref[...]
    mod_ref[...] = mod.astype(mod_ref.dtype)
    if mode == "moe":
        xh, xm = _split2(mod)
        nt = (((1,), (1,)), ((), ()))
        dot = functools.partial(lax.dot_general, dimension_numbers=nt, preferred_element_type=F32)
        lt_ref[...] = dot(wh_ref[...], xh) + (dot(wh_ref[...], xm) + dot(wm_ref[...], xh))


def _ln_call(mode, h, m=None, gate=None, ln_g=None, ln_b=None, sc=None, sh=None, w_router=None):
    rows, D = h.shape
    tm = LN_ROWS
    grp = lambda i: (i * tm // COND_ROWS, 0, 0)
    row_spec = pl.BlockSpec((tm, D), lambda i: (i, 0))
    cond_spec = pl.BlockSpec((None, 1, D), grp)
    vec_spec = pl.BlockSpec((1, D), lambda i: (0, 0))
    args, in_specs = [h], [row_spec]
    if mode != "first":
        args += [m, gate, ln_g, ln_b]
        in_specs += [row_spec, cond_spec, vec_spec, vec_spec]
    if mode != "last":
        args += [sc, sh]
        in_specs += [cond_spec, cond_spec]
    out_shape, out_specs = [], []
    if mode != "first":
        out_shape.append(jax.ShapeDtypeStruct((rows, D), F32))
        out_specs.append(row_spec)
    if mode != "last":
        out_shape.append(jax.ShapeDtypeStruct((rows, D), F32 if mode == "moe" else BF16))
        out_specs.append(row_spec)
    if mode == "moe":
        args += list(w_router)
        in_specs += [pl.BlockSpec((ROUTER_ROWS, D), lambda i: (0, 0))] * 2
        out_shape.append(jax.ShapeDtypeStruct((ROUTER_ROWS, rows), F32))
        out_specs.append(pl.BlockSpec((ROUTER_ROWS, tm), lambda i: (0, i)))
    return pl.pallas_call(
        functools.partial(_ln_kernel, mode=mode),
        grid=(rows // tm,),
        in_specs=in_specs,
        out_specs=out_specs,
        out_shape=out_shape,
        compiler_params=_cparams(1),
        name="ln_" + mode,
    )(*args)


def _hgrn_kernel(*refs, rev, ts, heads, has_state, emit_state):
    refs = list(refs)
    q_ref, i_ref, z_ref = refs[:3]
    pos = 3
    if rev:
        g_ref, of_ref = refs[pos:pos + 2]
        pos += 2
    tab_ref = refs[pos]
    pos += 1
    s0_ref = None
    if has_state:
        s0_ref = refs[pos]
        pos += 1
    o_ref = refs[pos]
    pos += 1
    sn_ref = None
    if emit_state:
        sn_ref = refs[pos]
        pos += 1
    st_ref = refs[pos]

    t = pl.program_id(2)
    n_t = pl.num_programs(2)
    C = HGRN_BLOCK
    n_chunks = ts // C
    dh = A_HEAD_DIM

    @pl.when(t == 0)
    def _():
        for hd in range(heads):
            if has_state:
                st_ref[hd] = s0_ref[hd].T
            else:
                st_ref[hd] = jnp.zeros((dh, dh), F32)

    row8 = lax.broadcasted_iota(jnp.int32, (SUBLANES, dh), 0)
    ti = lax.broadcasted_iota(jnp.int32, (C, C), 0)
    si = lax.broadcasted_iota(jnp.int32, (C, C), 1)
    tri = jnp.where((ti <= si) if rev else (ti >= si), 1.0, 0.0).astype(BF16)
    neg_inf = jnp.float32(-jnp.inf)
    base = 3 if rev else 0

    def stage_gates(rows, hd):
        cols = slice(hd * dh, (hd + 1) * dh)
        log_lb = tab_ref[base:base + 1, cols]
        log_1mlb = tab_ref[base + 1:base + 2, cols]
        one_mlb = tab_ref[base + 2:base + 3, cols]
        qr = q_ref[rows, cols]
        q = qr * _sigmoid(qr)
        v = i_ref[rows, cols]
        z = z_ref[rows, cols]
        e = jnp.exp(-jnp.abs(z))
        r = 1.0 / (1.0 + e)
        log_sig = jnp.minimum(z, 0.0) + jnp.log(r)
        k = one_mlb * (jnp.where(z >= 0.0, e, 1.0) * r)
        cc = log_1mlb + log_sig
        logf = jnp.maximum(log_lb, cc) + jnp.log1p(jnp.exp(-jnp.abs(log_lb - cc)))
        hi, mid, lo = _split3(logf)
        pieces = jnp.dot(tri, jnp.concatenate([hi, mid, lo], axis=1), preferred_element_type=F32)
        b = (pieces[:, :dh] + pieces[:, dh:2 * dh]) + pieces[:, 2 * dh:]
        return dict(q=q, k=k, v=v, b=b, cols=cols)

    def stage_state(hd, c):
        q, k, v, b = c["q"], c["k"], c["v"], c["b"]
        b_end = b[0:1, :] if rev else b[C - 1:C, :]
        st = st_ref[hd]
        qt = (q * jnp.exp(b)).astype(BF16)
        o = lax.dot_general(qt, st.astype(BF16), (((1,), (1,)), ((), ())), preferred_element_type=F32)
        kt = (k * jnp.exp(b_end - b)).astype(BF16)
        upd = lax.dot_general(v.astype(BF16), kt, (((0,), (0,)), ((), ())), preferred_element_type=F32)
        st_ref[hd] = st * jnp.exp(b_end) + upd
        c["halves"] = [o[0:SUBLANES, :], o[SUBLANES:C, :]]
        c["qh"] = [q[0:SUBLANES, :], q[SUBLANES:C, :]]
        c["bh"] = [b[0:SUBLANES, :], b[SUBLANES:C, :]]

    def stage_pair(s, c):
        b, k, v = c["b"], c["k"], c["v"]
        bs, ks, vs = b[s:s + 1, :], k[s:s + 1, :], v[s:s + 1, :]
        for h in range(2):
            lo_row, hi_row = h * SUBLANES, h * SUBLANES + SUBLANES - 1
            if rev:
                if lo_row > s:
                    continue
                full = hi_row <= s
            else:
                if hi_row < s:
                    continue
                full = lo_row >= s
            d = c["bh"][h] - bs
            if not full:
                keep = (row8 + lo_row <= s) if rev else (row8 + lo_row >= s)
                d = jnp.where(keep, d, neg_inf)
            p = (c["qh"][h] * ks) * jnp.exp(d)
            c["halves"][h] = c["halves"][h] + jnp.sum(p, axis=-1, keepdims=True) * vs

    def stage_out(rows, c):
        cols = c["cols"]
        o = jnp.concatenate(c["halves"], axis=0)
        if rev:
            tot = of_ref[rows, cols] + o
            nrm = tot * lax.rsqrt(jnp.mean(tot * tot, axis=-1, keepdims=True) + LN_EPS) * tab_ref[6:7, cols]
            gr = g_ref[rows, cols]
            o_ref[rows, cols] = (nrm * (gr * _sigmoid(gr))).astype(o_ref.dtype)
        else:
            o_ref[rows, cols] = o

    def body(ci, carry):
        c = (n_chunks - 1 - ci) if rev else ci
        rows = pl.ds(pl.multiple_of(c * C, C), C)
        ctx = [stage_gates(rows, hd) for hd in range(heads)]
        for hd in range(heads):
            stage_state(hd, ctx[hd])
        for s in range(C):
            for hd in range(heads):
                stage_pair(s, ctx[hd])
        for hd in range(heads):
            stage_out(rows, ctx[hd])
        return carry

    lax.fori_loop(0, n_chunks, body, 0)

    if emit_state:
        @pl.when(t == n_t - 1)
        def _():
            for hd in range(heads):
                sn_ref[hd] = st_ref[hd].T


def _hgrn_dir(proj, part, tab, state, layer_j, row0, n_seq, seq, rev, emit_state):
    has_state = state is not None
    G = HGRN_HEADS
    ts = min(HGRN_ROWS, seq)
    n_t = seq // ts
    hg_n = A_HEADS // G
    blk0 = row0 // ts
    tt = (lambda t: n_t - 1 - t) if rev else (lambda t: t)
    col = lambda k: (lambda b, hg, t: (blk0 + b * n_t + tt(t), k * hg_n + hg))
    in_specs = [pl.BlockSpec((ts, G * A_HEAD_DIM), col(k)) for k in (0, 1, 3 if rev else 2)]
    args = [proj, proj, proj]
    if rev:
        in_specs += [pl.BlockSpec((ts, G * A_HEAD_DIM), col(4)),
                     pl.BlockSpec((ts, G * A_HEAD_DIM), lambda b, hg, t: (b * n_t + tt(t), hg))]
        args += [proj, part]
    in_specs.append(pl.BlockSpec((SUBLANES, G * A_HEAD_DIM), lambda b, hg, t: (0, hg)))
    args.append(tab)
    if has_state:
        in_specs.append(pl.BlockSpec((None, None, None, G, A_HEAD_DIM, A_HEAD_DIM),
                                     lambda b, hg, t: (b, layer_j, 1 if rev else 0, hg, 0, 0)))
        args.append(state)
    out_specs = [pl.BlockSpec((ts, G * A_HEAD_DIM), lambda b, hg, t: (b * n_t + tt(t), hg))]
    out_shape = [jax.ShapeDtypeStruct((n_seq * seq, A_WIDTH), BF16 if rev else F32)]
    if emit_state:
        out_specs.append(pl.BlockSpec((None, G, A_HEAD_DIM, A_HEAD_DIM), lambda b, hg, t: (b, hg, 0, 0)))
        out_shape.append(jax.ShapeDtypeStruct((n_seq, A_HEADS, A_HEAD_DIM, A_HEAD_DIM), F32))
    return pl.pallas_call(
        functools.partial(_hgrn_kernel, rev=rev, ts=ts, heads=G, has_state=has_state, emit_state=emit_state),
        grid=(n_seq, hg_n, n_t),
        in_specs=in_specs,
        out_specs=out_specs,
        out_shape=out_shape,
        scratch_shapes=[pltpu.VMEM((G, A_HEAD_DIM, A_HEAD_DIM), F32)],
        compiler_params=_cparams(3),
        name="hgrn_bwd" if rev else "hgrn_fwd",
    )(*args)


def _hgrn(proj, tab, state, layer_j, row0, n_seq, seq, emit_state):
    fwd = _hgrn_dir(proj, None, tab, state, layer_j, row0, n_seq, seq, False, emit_state)
    bwd = _hgrn_dir(proj, fwd[0], tab, state, layer_j, row0, n_seq, seq, True, emit_state)
    if emit_state:
        return bwd[0], jnp.stack([fwd[1], bwd[1]], axis=1)
    return bwd[0], None


def _gelu_tanh(x):
    return 0.5 * x * (1.0 + jnp.tanh(math.sqrt(2.0 / math.pi) * (x + 0.044715 * (x * x * x))))


def _sgu_kernel(u_ref, v_ref, lng_ref, lnb_ref, w_ref, bias_ref, o_ref):
    v = _gelu_tanh(v_ref[...])
    mu = jnp.mean(v, axis=-1, keepdims=True)
    vc = v - mu
    var = jnp.mean(vc * vc, axis=-1, keepdims=True)
    vn = (vc * lax.rsqrt(var + LN_EPS) * lng_ref[...] + lnb_ref[...]).astype(BF16)
    for g in range(B_GROUPS):
        sl = slice(g * SGU_CHUNK, (g + 1) * SGU_CHUNK)
        mixed = jnp.dot(w_ref[g], vn[:, sl], preferred_element_type=F32) + bias_ref[:, sl]
        o_ref[:, sl] = (_gelu_tanh(u_ref[:, sl]) * mixed).astype(o_ref.dtype)


def _sgu(proj, ln_g, ln_b, w_s, bias_full):
    rows = proj.shape[0]
    ublk = 5 * A_WIDTH // B_WIDTH
    return pl.pallas_call(
        _sgu_kernel,
        grid=(rows // SGU_CHUNK,),
        in_specs=[pl.BlockSpec((SGU_CHUNK, B_WIDTH), lambda r: (r, ublk)),
                  pl.BlockSpec((SGU_CHUNK, B_WIDTH), lambda r: (r, ublk + 1)),
                  pl.BlockSpec((1, B_WIDTH), lambda r: (0, 0)),
                  pl.BlockSpec((1, B_WIDTH), lambda r: (0, 0)),
                  pl.BlockSpec((B_GROUPS, SGU_CHUNK, SGU_CHUNK), lambda r: (0, 0, 0)),
                  pl.BlockSpec((SGU_CHUNK, B_WIDTH), lambda r: (0, 0))],
        out_specs=pl.BlockSpec((SGU_CHUNK, B_WIDTH), lambda r: (r, 0)),
        out_shape=jax.ShapeDtypeStruct((rows, B_WIDTH), BF16),
        compiler_params=_cparams(1),
        name="sgu",
    )(proj, proj, ln_g, ln_b, w_s, bias_full)


def _attn_kernel(*refs, with_cache, kb):
    if with_cache:
        (q_ref, cos_ref, sin_ref, kv_ref, kpe_ref, kvc_ref, kpec_ref, o_ref,
         kx_ref, vx_ref, s_ref, kxc_ref, vxc_ref, sc_ref) = refs
    else:
        q_ref, cos_ref, sin_ref, kv_ref, kpe_ref, o_ref, kx_ref, vx_ref, s_ref = refs
    nt = (((1,), (1,)), ((), ()))
    pair = NOPE_DIM + V_DIM
    qw = NOPE_DIM + 2 * ROPE_DIM
    tq = q_ref.shape[0]
    seq = kv_ref.shape[0]

    @pl.when(pl.program_id(2) == 0)
    def _():
        pad = qw - NOPE_DIM - ROPE_DIM
        for i in range(2):
            kx_ref[i, :, 0:NOPE_DIM] = kv_ref[:, i * pair:i * pair + NOPE_DIM]
            kx_ref[i, :, NOPE_DIM:NOPE_DIM + ROPE_DIM] = kpe_ref[...]
            kx_ref[i, :, NOPE_DIM + ROPE_DIM:qw] = jnp.zeros((seq, pad), BF16)
            vx_ref[i, :, 0:V_DIM] = kv_ref[:, i * pair + NOPE_DIM:(i + 1) * pair]
            vx_ref[i, :, V_DIM:2 * V_DIM] = jnp.ones((seq, V_DIM), BF16)
            if with_cache:
                past = kvc_ref.shape[0]
                kxc_ref[i, :, 0:NOPE_DIM] = kvc_ref[:, i * pair:i * pair + NOPE_DIM]
                kxc_ref[i, :, NOPE_DIM:NOPE_DIM + ROPE_DIM] = kpec_ref[...]
                kxc_ref[i, :, NOPE_DIM + ROPE_DIM:qw] = jnp.zeros((past, pad), BF16)
                vxc_ref[i, :, 0:V_DIM] = kvc_ref[:, i * pair + NOPE_DIM:(i + 1) * pair]
                vxc_ref[i, :, V_DIM:2 * V_DIM] = jnp.ones((past, V_DIM), BF16)

    cos, sin = cos_ref[...], sin_ref[...]
    n_kb = seq // kb
    qs = []
    for i in range(2):
        q0 = i * qw
        qpe = (q_ref[:, q0 + NOPE_DIM:q0 + NOPE_DIM + ROPE_DIM].astype(F32) * cos
               + q_ref[:, q0 + NOPE_DIM + ROPE_DIM:q0 + qw].astype(F32) * sin).astype(BF16)
        qs.append(jnp.concatenate([q_ref[:, q0:q0 + NOPE_DIM], qpe,
                                   jnp.zeros((tq, qw - NOPE_DIM - ROPE_DIM), BF16)], axis=1))
    m = [None, None]
    for j in range(n_kb):
        for i in range(2):
            s = lax.dot_general(qs[i], kx_ref[i, j * kb:(j + 1) * kb, :], nt, preferred_element_type=F32)
            s_ref[i, :, j * kb:(j + 1) * kb] = s
            bm = jnp.max(s, axis=-1, keepdims=True)
            m[i] = bm if m[i] is None else jnp.maximum(m[i], bm)
    if with_cache:
        for i in range(2):
            s = lax.dot_general(qs[i], kxc_ref[i], nt, preferred_element_type=F32)
            sc_ref[i] = s
            m[i] = jnp.maximum(m[i], jnp.max(s, axis=-1, keepdims=True))
    acc = [jnp.zeros((tq, 2 * V_DIM), F32), jnp.zeros((tq, 2 * V_DIM), F32)]
    for j in range(n_kb):
        for i in range(2):
            p = jnp.exp2(s_ref[i, :, j * kb:(j + 1) * kb] - m[i]).astype(BF16)
            acc[i] = acc[i] + jnp.dot(p, vx_ref[i, j * kb:(j + 1) * kb, :], preferred_element_type=F32)
    for i in range(2):
        a = acc[i]
        if with_cache:
            a = a + jnp.dot(jnp.exp2(sc_ref[i] - m[i]).astype(BF16), vxc_ref[i], preferred_element_type=F32)
        o_ref[:, i * V_DIM:(i + 1) * V_DIM] = (a[:, 0:V_DIM] / a[:, V_DIM:V_DIM + 1]).astype(o_ref.dtype)


def _attn(q, cos_t, sin_t, kv, kpe, row0, n_seq, seq, tq, cache=None):
    nq = seq // tq
    qb0, kb0 = row0 // tq, row0 // seq
    pair = 2 * (NOPE_DIM + V_DIM)
    qw = NOPE_DIM + 2 * ROPE_DIM
    kb = min(ATTN_KB, seq)
    qrow = lambda b, h, i: (qb0 + b * nq + i, 0)
    in_specs = [pl.BlockSpec((tq, 2 * qw), lambda b, h, i: (qb0 + b * nq + i, h)),
                pl.BlockSpec((tq, ROPE_DIM), qrow),
                pl.BlockSpec((tq, ROPE_DIM), qrow),
                pl.BlockSpec((seq, pair), lambda b, h, i: (kb0 + b, h)),
                pl.BlockSpec((seq, ROPE_DIM), lambda b, h, i: (kb0 + b, 0))]
    args = [q, cos_t, sin_t, kv, kpe]
    scratch = [pltpu.VMEM((2, seq, qw), BF16), pltpu.VMEM((2, seq, 2 * V_DIM), BF16), pltpu.VMEM((2, tq, seq), F32)]
    if cache is not None:
        kvc, kpec, past = cache
        in_specs += [pl.BlockSpec((past, pair), lambda b, h, i: (b, h)),
                     pl.BlockSpec((past, ROPE_DIM), lambda b, h, i: (b, 0))]
        args += [kvc, kpec]
        scratch += [pltpu.VMEM((2, past, qw), BF16), pltpu.VMEM((2, past, 2 * V_DIM), BF16),
                    pltpu.VMEM((2, tq, past), F32)]
    return pl.pallas_call(
        functools.partial(_attn_kernel, with_cache=cache is not None, kb=kb),
        grid=(n_seq, C_HEADS // 2, nq),
        in_specs=in_specs,
        out_specs=pl.BlockSpec((tq, 2 * V_DIM), lambda b, h, i: (b * nq + i, h)),
        out_shape=jax.ShapeDtypeStruct((n_seq * seq, C_HEADS * V_DIM), BF16),
        scratch_shapes=scratch,
        compiler_params=_cparams(3),
        name="mla_attn",
    )(*args)


def _moe_up_kernel(be_ref, nu_ref, x_ref, wg_ref, wu_ref, h_ref, wgb_ref, wub_ref):
    i = pl.program_id(1)
    fresh = jnp.logical_or(i == 0, be_ref[i] != be_ref[jnp.maximum(i - 1, 0)])

    @pl.when(fresh)
    def _():
        wgb_ref[...] = wg_ref[...].astype(BF16)
        wub_ref[...] = wu_ref[...].astype(BF16)

    @pl.when(i < nu_ref[0])
    def _():
        x = x_ref[...].astype(BF16)
        a = jnp.dot(x, wgb_ref[...], preferred_element_type=F32)
        b = jnp.dot(x, wub_ref[...], preferred_element_type=F32)
        h_ref[...] = (a * _sigmoid(a) * b).astype(h_ref.dtype)

    @pl.when(i >= nu_ref[0])
    def _():
        h_ref[...] = jnp.zeros_like(h_ref)


def _moe_down_kernel(be_ref, nu_ref, h_ref, wd_ref, y_ref, wdb_ref):
    i = pl.program_id(1)
    fresh = jnp.logical_or(i == 0, be_ref[i] != be_ref[jnp.maximum(i - 1, 0)])

    @pl.when(fresh)
    def _():
        wdb_ref[...] = wd_ref[...].astype(BF16)

    @pl.when(i < nu_ref[0])
    def _():
        y_ref[...] = jnp.dot(h_ref[...], wdb_ref[...], preferred_element_type=F32)

    @pl.when(i >= nu_ref[0])
    def _():
        y_ref[...] = jnp.zeros_like(y_ref)


def _moe_ffn(xp, blk_expert, n_used, w_gate, w_up, w_down, layer):
    n_rows, D = xp.shape
    FF = w_gate.shape[-1]
    n_blk = n_rows // MOE_TM
    h = pl.pallas_call(
        _moe_up_kernel,
        grid_spec=pltpu.PrefetchScalarGridSpec(
            num_scalar_prefetch=2,
            grid=(FF // MOE_TF, n_blk),
            in_specs=[pl.BlockSpec((MOE_TM, D), lambda f, i, be, nu: (i, 0)),
                      pl.BlockSpec((None, None, D, MOE_TF), lambda f, i, be, nu: (layer, be[i], 0, f)),
                      pl.BlockSpec((None, None, D, MOE_TF), lambda f, i, be, nu: (layer, be[i], 0, f))],
            out_specs=pl.BlockSpec((MOE_TM, MOE_TF), lambda f, i, be, nu: (i, f)),
            scratch_shapes=[pltpu.VMEM((D, MOE_TF), BF16), pltpu.VMEM((D, MOE_TF), BF16)]),
        out_shape=jax.ShapeDtypeStruct((n_rows, FF), BF16),
        compiler_params=_cparams(2),
        name="moe_up",
    )(blk_expert, n_used, xp, w_gate, w_up)
    return pl.pallas_call(
        _moe_down_kernel,
        grid_spec=pltpu.PrefetchScalarGridSpec(
            num_scalar_prefetch=2,
            grid=(D // MOE_TN, n_blk),
            in_specs=[pl.BlockSpec((MOE_TM, FF), lambda n, i, be, nu: (i, 0)),
                      pl.BlockSpec((None, None, FF, MOE_TN), lambda n, i, be, nu: (layer, be[i], 0, n))],
            out_specs=pl.BlockSpec((MOE_TM, MOE_TN), lambda n, i, be, nu: (i, n)),
            scratch_shapes=[pltpu.VMEM((FF, MOE_TN), BF16)]),
        out_shape=jax.ShapeDtypeStruct((n_rows, D), F32),
        compiler_params=_cparams(2),
        name="moe_down",
    )(blk_expert, n_used, h, w_down)


def _route(logits_t, b_router_t):
    n_tok = logits_t.shape[1]
    scores = jax.nn.sigmoid(logits_t).reshape(N_GROUPS, SUBLANES, n_tok)
    r_io = lax.broadcasted_iota(jnp.int32, (N_GROUPS, SUBLANES, n_tok), 1)
    g_io = lax.broadcasted_iota(jnp.int32, (N_GROUPS, n_tok), 0)
    sel = jnp.where(r_io < EXPERTS_PER_GROUP, scores + b_router_t.reshape(N_GROUPS, SUBLANES, 1), -jnp.inf)
    m1 = jnp.max(sel, axis=1)
    i1 = jnp.min(jnp.where(sel == m1[:, None, :], r_io, SUBLANES), axis=1)
    sel2 = jnp.where(r_io == i1[:, None, :], -jnp.inf, sel)
    m2 = jnp.max(sel2, axis=1)
    i2 = jnp.min(jnp.where(sel2 == m2[:, None, :], r_io, SUBLANES), axis=1)
    group_score = m1 + m2
    g_idx = jnp.min(jnp.where(group_score == jnp.max(group_score, axis=0, keepdims=True), g_io, N_GROUPS), axis=0)
    best = g_io == g_idx[None, :]
    l1 = jnp.sum(jnp.where(best, i1, 0), axis=0)
    l2 = jnp.sum(jnp.where(best, i2, 0), axis=0)
    local = jnp.stack([l1, l2], axis=0)
    pick = best[None, :, None, :] & (r_io[None] == local[:, None, None, :])
    s = jnp.sum(jnp.where(pick, scores[None], 0.0), axis=(1, 2))
    gate = s / jnp.sum(s, axis=0, keepdims=True)
    return (g_idx[None, :] * EXPERTS_PER_GROUP + local).astype(jnp.int32), gate


def _dispatch(expert_idx):
    n_tok = expert_idx.shape[1]
    n_flat = TOP_K * n_tok
    nb = n_flat // LANES
    e_flat = expert_idx.reshape(nb, LANES)
    onehot = e_flat[None] == lax.broadcasted_iota(jnp.int32, (N_EXPERTS, nb, LANES), 0)
    upper = (lax.broadcasted_iota(jnp.int32, (LANES, LANES), 0)
             <= lax.broadcasted_iota(jnp.int32, (LANES, LANES), 1)).astype(BF16)
    within = jnp.einsum('ebl,lm->ebm', onehot.astype(BF16), upper, preferred_element_type=F32)
    blk_cnt = within[:, :, LANES - 1]
    blk_end = jnp.cumsum(blk_cnt, axis=1)
    counts = blk_end[:, -1].astype(jnp.int32)
    padded = (counts + MOE_TM - 1) // MOE_TM * MOE_TM
    pad_end = jnp.cumsum(padded)
    pad_start = pad_end - padded
    offs = (blk_end - blk_cnt)[:, :, None] + pad_start.astype(F32)[:, None, None] + within - 1.0
    pos = jnp.sum(jnp.where(onehot, offs, 0.0), axis=0).astype(jnp.int32).reshape(n_flat)
    n_blk = (n_flat + N_EXPERTS * (MOE_TM - 1) + MOE_TM - 1) // MOE_TM
    blk_expert = jnp.minimum(jnp.searchsorted(pad_end, jnp.arange(n_blk, dtype=jnp.int32) * MOE_TM, side='right'),
                             N_EXPERTS - 1).astype(jnp.int32)
    n_used = (pad_end[-1:] // MOE_TM).astype(jnp.int32)
    tok = lax.rem(jnp.arange(n_flat, dtype=jnp.int32), n_tok)
    src = jnp.zeros((n_blk * MOE_TM,), jnp.int32).at[pos].set(tok, unique_indices=True)
    return pos.reshape(TOP_K, n_tok), blk_expert, n_used, src


def _moe(x, logits_t, b_router_t, w_gate, w_up, w_down, layer):
    expert_idx, gate = _route(logits_t, b_router_t)
    pos, blk_expert, n_used, src = _dispatch(expert_idx)
    xp = jnp.take(x, src, axis=0)
    yp = _moe_ffn(xp, blk_expert, n_used, w_gate, w_up, w_down, layer)
    return yp[pos[0]] * gate[0][:, None] + yp[pos[1]] * gate[1][:, None]


def _rmsnorm(x, g):
    return x * lax.rsqrt(jnp.mean(x * x, axis=-1, keepdims=True) + LN_EPS) * g


def _rot_cols(w):
    shp = w.shape
    w4 = w.reshape(shp[:-1] + (shp[-1] // 32, 2, 16))
    return jnp.stack([-w4[..., 1, :], w4[..., 0, :]], axis=-2).reshape(shp)


def _rope_tables(n_prompt_rows, n_seq, seq):
    t = jnp.arange(seq)
    row = (t // GRID_W).astype(F32)
    colp = (t % GRID_W).astype(F32)
    n_freq = ROPE_DIM // 4
    inv = ROPE_THETA ** (-jnp.arange(n_freq, dtype=F32) / n_freq)
    ang = jnp.stack([row[:, None] * inv, colp[:, None] * inv], axis=1)
    cos = jnp.broadcast_to(jnp.cos(ang)[:, :, None, :], (seq, 2, 2, n_freq)).reshape(seq, ROPE_DIM)
    sin = jnp.broadcast_to(jnp.sin(ang)[:, :, None, :], (seq, 2, 2, n_freq)).reshape(seq, ROPE_DIM)
    cos = jnp.concatenate([jnp.ones((n_prompt_rows, ROPE_DIM), F32), jnp.tile(cos, (n_seq, 1))], axis=0)
    sin = jnp.concatenate([jnp.zeros((n_prompt_rows, ROPE_DIM), F32), jnp.tile(sin, (n_seq, 1))], axis=0)
    return cos, sin


def _router_weights(w_router, b_router):
    D = w_router.shape[0]
    wt = w_router.T.reshape(N_GROUPS, EXPERTS_PER_GROUP, D)
    wt = jnp.pad(wt, ((0, 0), (0, SUBLANES - EXPERTS_PER_GROUP), (0, 0))).reshape(ROUTER_ROWS, D)
    bt = jnp.pad(b_router.astype(F32).reshape(N_GROUPS, EXPERTS_PER_GROUP),
                 ((0, 0), (0, SUBLANES - EXPERTS_PER_GROUP))).reshape(ROUTER_ROWS)
    return _split2(wt.astype(F32)), bt


def kernel(x_prompt, x_sample, state_hgrn, cache_ckv, cache_kpe, c, c_ctx, w_ada, b_ada, ln_g, ln_b, w_in_ab, hgrn_lb, hgrn_norm_g, sgu_ln_g, sgu_ln_b, sgu_w, sgu_b, w_out_ab, w_in_c, q_norm_g, kv_norm_g, w_uq, w_ukv, w_o_c, w_router, b_router, w_gate_e, w_up_e, w_down_e):
    n_p, s_p, D = x_prompt.shape
    n_s, s_s, _ = x_sample.shape
    past = cache_ckv.shape[2]
    rows_p, rows_s = n_p * s_p, n_s * s_s
    rows = rows_p + rows_s
    assert rows_p % COND_ROWS == 0 and s_s == COND_ROWS
    n_grp = rows // COND_ROWS
    grp_p = rows_p // COND_ROWS

    h = jnp.concatenate([x_prompt.reshape(rows_p, D), x_sample.reshape(rows_s, D)], axis=0)

    cond = jnp.concatenate([jnp.broadcast_to(c_ctx[None, :], (grp_p, D)), c], axis=0)
    cond_rows = -(-n_grp // 16) * 16
    cond = jnp.pad(jax.nn.silu(cond.astype(F32)), ((0, cond_rows - n_grp), (0, 0))).astype(BF16)
    b_ada3 = b_ada.reshape(DEPTH, 1, ADA_MULT * D)

    def modulation(layer):
        mod = _ada_mm(cond, w_ada, b_ada3, layer)[:n_grp]
        return [m[:, None, :] for m in jnp.split(mod, ADA_MULT, axis=-1)]

    p = jax.nn.softmax(hgrn_lb.astype(F32), axis=0)
    lbs = jnp.cumsum(p, axis=0) - p[0:1]

    cos_t, sin_t = _rope_tables(rows_p, n_s, s_s)
    w_router_t, b_router_t = _router_weights(w_router, b_router)
    q_scale = math.log2(math.e) / math.sqrt(NOPE_DIM + ROPE_DIM)

    hg_states, ckv_list, kpe_list = [], [], []
    sh1, sc1, g1, sh2, sc2, g2 = modulation(0)
    (a,) = _ln_call("first", h, sc=sc1, sh=sh1)
    for layer in range(DEPTH):
        j = layer // 2
        if layer % 2 == 0:
            proj = _mm(a, w_in_ab[j].astype(BF16), F32)
            lb = lbs[j]
            tab = jnp.concatenate([jnp.log(lb[0:1]), jnp.log1p(-lb[0:1]), 1.0 - lb[0:1],
                                   jnp.log(lb[1:2]), jnp.log1p(-lb[1:2]), 1.0 - lb[1:2],
                                   hgrn_norm_g[j][None, :], jnp.zeros((1, A_WIDTH), F32)], axis=0)
            oa_p, st_p = _hgrn(proj, tab, None, j, 0, n_p, s_p, True)
            oa_s, _ = _hgrn(proj, tab, state_hgrn, j, rows_p, n_s, s_s, False)
            hg_states.append(st_p)
            bias_full = jnp.repeat(sgu_b[j].T, SGU_CHUNK, axis=1)
            o_sgu = _sgu(proj, sgu_ln_g[j][None, :], sgu_ln_b[j][None, :], sgu_w[j].astype(BF16), bias_full)
            m = _mm2(jnp.concatenate([oa_p, oa_s], axis=0), o_sgu, w_out_ab[j].astype(BF16), F32)
        else:
            w_kpe = w_in_c[j][:, Q_LORA + KV_LORA:]
            n_c = Q_LORA + KV_LORA + 2 * ROPE_DIM
            w_in_ext = jnp.concatenate([w_in_c[j], _rot_cols(w_kpe), jnp.zeros((D, 2048 - n_c), F32)], axis=1)
            pc = _mm(a, w_in_ext.astype(BF16), F32)
            cq = _rmsnorm(pc[:, :Q_LORA], q_norm_g[j]).astype(BF16)
            ckv = _rmsnorm(pc[:, Q_LORA:Q_LORA + KV_LORA], kv_norm_g[j])
            kpe_raw = pc[:, Q_LORA + KV_LORA:Q_LORA + KV_LORA + ROPE_DIM]
            kpe_rot = pc[:, Q_LORA + KV_LORA + ROPE_DIM:n_c]
            kpe = (kpe_raw * cos_t + kpe_rot * sin_t).astype(BF16)
            ckv_list.append(ckv[:rows_p].reshape(n_p, s_p, KV_LORA))
            kpe_list.append(kpe_raw[:rows_p].reshape(n_p, s_p, ROPE_DIM))
            wq = w_uq[j].reshape(Q_LORA, C_HEADS, NOPE_DIM + ROPE_DIM) * q_scale
            wq_p = wq[:, :, NOPE_DIM:]
            wq_ext = jnp.concatenate([wq[:, :, :NOPE_DIM], wq_p, _rot_cols(wq_p)], axis=2)
            q = _mm(cq, wq_ext.reshape(Q_LORA, C_HEADS * (NOPE_DIM + 2 * ROPE_DIM)).astype(BF16), BF16)
            w_ukv_b = w_ukv[j].astype(BF16)
            kv = _mm(ckv.astype(BF16), w_ukv_b, BF16)
            kvc = _mm(cache_ckv[:, j].reshape(n_s * past, KV_LORA).astype(BF16), w_ukv_b, BF16)
            kpec = cache_kpe[:, j].reshape(n_s * past, ROPE_DIM).astype(BF16)
            o_p = _attn(q, cos_t, sin_t, kv, kpe, 0, n_p, s_p, s_p)
            o_s = _attn(q, cos_t, sin_t, kv, kpe, rows_p, n_s, s_s, ATTN_TQ, cache=(kvc, kpec, past))
            m = _mm(jnp.concatenate([o_p, o_s], axis=0), w_o_c[j].astype(BF16), F32)
        h, bmod, logits_t = _ln_call("moe", h, m, g1, ln_g[layer, 0][None, :], ln_b[layer, 0][None, :],
                                     sc2, sh2, w_router_t)
        y = _moe(bmod, logits_t, b_router_t, w_gate_e, w_up_e, w_down_e, layer)
        if layer + 1 < DEPTH:
            sh1, sc1, g1, sh2n, sc2n, g2n = modulation(layer + 1)
            h, a = _ln_call("mixer", h, y, g2, ln_g[layer, 1][None, :], ln_b[layer, 1][None, :], sc1, sh1)
            sh2, sc2, g2 = sh2n, sc2n, g2n
        else:
            (h,) = _ln_call("last", h, y, g2, ln_g[layer, 1][None, :], ln_b[layer, 1][None, :])

    y_prompt = h[:rows_p].reshape(n_p, s_p, D)
    y_sample = h[rows_p:].reshape(n_s, s_s, D)
    return (y_prompt, y_sample, jnp.stack(hg_states, axis=1), jnp.stack(ckv_list, axis=1), jnp.stack(kpe_list, axis=1))
```

```python
import functools
import math

import jax
import jax.numpy as jnp
from jax import lax
from jax.experimental import pallas as pl
from jax.experimental.pallas import tpu as pltpu

F32 = jnp.float32
BF16 = jnp.bfloat16

D_MODEL = 4096
DEPTH = 4
GRID_W = 64
A_WIDTH = D_MODEL // 2
A_HEAD_DIM = 128
A_HEADS = A_WIDTH // A_HEAD_DIM
B_WIDTH = D_MODEL // 2
SGU_CHUNK = 128
B_GROUPS = 16
C_HEADS = D_MODEL // 128
Q_LORA = D_MODEL // 4
KV_LORA = 512
NOPE_DIM = 128
ROPE_DIM = 64
V_DIM = 128
ROPE_THETA = 10000.0
N_EXPERTS = 48
N_GROUPS = 8
EXPERTS_PER_GROUP = N_EXPERTS // N_GROUPS
TOP_K = 2
EXPERT_FF = D_MODEL // 4
ADA_MULT = 6
DEEPNORM_ALPHA = (2.0 * DEPTH) ** 0.25
LN_EPS = 1e-6

LANES = 128
SUBLANES = 8
COND_ROWS = 4096
HGRN_BLOCK = 16
HGRN_HEADS = 16
HGRN_ROWS = 256
LN_ROWS = 256
ROUTER_ROWS = N_GROUPS * SUBLANES
ATTN_TQ = 512
ATTN_KB = 512
MOE_TM = 512
MOE_TF = 256
MOE_TN = 1024
VMEM_LIMIT = 56 * 1024 * 1024


def _cparams(n_axes):
    return pltpu.CompilerParams(dimension_semantics=("arbitrary",) * n_axes, vmem_limit_bytes=VMEM_LIMIT)


def _split2(a):
    hi = a.astype(BF16)
    return hi, (a - hi.astype(F32)).astype(BF16)


def _split3(a):
    hi = a.astype(BF16)
    r1 = a - hi.astype(F32)
    mid = r1.astype(BF16)
    lo = (r1 - mid.astype(F32)).astype(BF16)
    return hi, mid, lo


def _sigmoid(x):
    return 1.0 / (1.0 + jnp.exp(-x))


def _mm_kernel(x_ref, w_ref, o_ref):
    o_ref[...] = jnp.dot(x_ref[...], w_ref[...], preferred_element_type=F32).astype(o_ref.dtype)


def _mm(x, w, out_dtype, tm=1024, tn=1024):
    M, K = x.shape
    N = w.shape[1]
    tm, tn = min(tm, M), min(tn, N)
    assert M % tm == 0 and N % tn == 0
    return pl.pallas_call(
        _mm_kernel,
        grid=(N // tn, M // tm),
        in_specs=[pl.BlockSpec((tm, K), lambda n, m: (m, 0)),
                  pl.BlockSpec((K, tn), lambda n, m: (0, n))],
        out_specs=pl.BlockSpec((tm, tn), lambda n, m: (m, n)),
        out_shape=jax.ShapeDtypeStruct((M, N), out_dtype),
        compiler_params=_cparams(2),
        name="dense_mm",
    )(x, w)


def _mm2_kernel(x1_ref, x2_ref, w1_ref, w2_ref, o_ref):
    o_ref[...] = (jnp.dot(x1_ref[...], w1_ref[...], preferred_element_type=F32)
                  + jnp.dot(x2_ref[...], w2_ref[...], preferred_element_type=F32)).astype(o_ref.dtype)


def _mm2(x1, x2, w, out_dtype, tm=1024, tn=1024):
    M, K1 = x1.shape
    K2 = x2.shape[1]
    N = w.shape[1]
    assert K1 == K2 and w.shape[0] == K1 + K2
    return pl.pallas_call(
        _mm2_kernel,
        grid=(N // tn, M // tm),
        in_specs=[pl.BlockSpec((tm, K1), lambda n, m: (m, 0)),
                  pl.BlockSpec((tm, K2), lambda n, m: (m, 0)),
                  pl.BlockSpec((K1, tn), lambda n, m: (0, n)),
                  pl.BlockSpec((K2, tn), lambda n, m: (1, n))],
        out_specs=pl.BlockSpec((tm, tn), lambda n, m: (m, n)),
        out_shape=jax.ShapeDtypeStruct((M, N), out_dtype),
        compiler_params=_cparams(2),
        name="dense_mm2",
    )(x1, x2, w, w)


def _ada_kernel(x_ref, w_ref, b_ref, o_ref):
    o_ref[...] = jnp.dot(x_ref[...], w_ref[...].astype(BF16), preferred_element_type=F32) + b_ref[...]


def _ada_mm(x, w_all, b_all, layer, tn=1024):
    R, K = x.shape
    N = w_all.shape[2]
    return pl.pallas_call(
        _ada_kernel,
        grid=(N // tn,),
        in_specs=[pl.BlockSpec((R, K), lambda n: (0, 0)),
                  pl.BlockSpec((None, K, tn), lambda n: (layer, 0, n)),
                  pl.BlockSpec((None, 1, tn), lambda n: (layer, 0, n))],
        out_specs=pl.BlockSpec((R, tn), lambda n: (0, n)),
        out_shape=jax.ShapeDtypeStruct((R, N), F32),
        compiler_params=_cparams(1),
        name="ada_mm",
    )(x, w_all, b_all)


def _ln_kernel(*refs, mode):
    if mode == "first":
        h_ref, sc_ref, sh_ref, mod_ref = refs
        hn = h_ref[...]
    else:
        h_ref, m_ref, gate_ref, lng_ref, lnb_ref = refs[:5]
        x = DEEPNORM_ALPHA * h_ref[...] + gate_ref[...] * m_ref[...]
        mu = jnp.mean(x, axis=-1, keepdims=True)
        xc = x - mu
        var = jnp.mean(xc * xc, axis=-1, keepdims=True)
        hn = xc * lax.rsqrt(var + LN_EPS) * lng_ref[...] + lnb_ref[...]
        if mode == "last":
            refs[5][...] = hn
            return
        sc_ref, sh_ref = refs[5:7]
        if mode == "moe":
            wh_ref, wm_ref, hn_ref, mod_ref, lt_ref = refs[7:]
        else:
            hn_ref, mod_ref = refs[7:]
        hn_ref[...] = hn
    mod = hn * (1.0 + sc_ref[...]) + sh_ref[...]
    mod_ref[...] = mod.astype(mod_ref.dtype)
    if mode == "moe":
        xh, xm = _split2(mod)
        nt = (((1,), (1,)), ((), ()))
        dot = functools.partial(lax.dot_general, dimension_numbers=nt, preferred_element_type=F32)
        lt_ref[...] = dot(wh_ref[...], xh) + (dot(wh_ref[...], xm) + dot(wm_ref[...], xh))


def _ln_call(mode, h, m=None, gate=None, ln_g=None, ln_b=None, sc=None, sh=None, w_router=None):
    rows, D = h.shape
    tm = LN_ROWS
    grp = lambda i: (i * tm // COND_ROWS, 0, 0)
    row_spec = pl.BlockSpec((tm, D), lambda i: (i, 0))
    cond_spec = pl.BlockSpec((None, 1, D), grp)
    vec_spec = pl.BlockSpec((1, D), lambda i: (0, 0))
    args, in_specs = [h], [row_spec]
    if mode != "first":
        args += [m, gate, ln_g, ln_b]
        in_specs += [row_spec, cond_spec, vec_spec, vec_spec]
    if mode != "last":
        args += [sc, sh]
        in_specs += [cond_spec, cond_spec]
    out_shape, out_specs = [], []
    if mode != "first":
        out_shape.append(jax.ShapeDtypeStruct((rows, D), F32))
        out_specs.append(row_spec)
    if mode != "last":
        out_shape.append(jax.ShapeDtypeStruct((rows, D), BF16))
        out_specs.append(row_spec)
    if mode == "moe":
        args += list(w_router)
        in_specs += [pl.BlockSpec((ROUTER_ROWS, D), lambda i: (0, 0))] * 2
        out_shape.append(jax.ShapeDtypeStruct((ROUTER_ROWS, rows), F32))
        out_specs.append(pl.BlockSpec((ROUTER_ROWS, tm), lambda i: (0, i)))
    return pl.pallas_call(
        functools.partial(_ln_kernel, mode=mode),
        grid=(rows // tm,),
        in_specs=in_specs,
        out_specs=out_specs,
        out_shape=out_shape,
        compiler_params=_cparams(1),
        name="ln_" + mode,
    )(*args)


def _hgrn_kernel(*refs, rev, ts, heads, has_state, emit_state):
    refs = list(refs)
    q_ref, i_ref, z_ref = refs[:3]
    pos = 3
    if rev:
        g_ref, of_ref = refs[pos:pos + 2]
        pos += 2
    tab_ref = refs[pos]
    pos += 1
    s0_ref = None
    if has_state:
        s0_ref = refs[pos]
        pos += 1
    o_ref = refs[pos]
    pos += 1
    sn_ref = None
    if emit_state:
        sn_ref = refs[pos]
        pos += 1
    st_ref = refs[pos]

    t = pl.program_id(2)
    n_t = pl.num_programs(2)
    C = HGRN_BLOCK
    n_chunks = ts // C
    dh = A_HEAD_DIM

    @pl.when(t == 0)
    def _():
        for hd in range(heads):
            if has_state:
                st_ref[hd] = s0_ref[hd].T
            else:
                st_ref[hd] = jnp.zeros((dh, dh), F32)

    row8 = lax.broadcasted_iota(jnp.int32, (SUBLANES, dh), 0)
    ti = lax.broadcasted_iota(jnp.int32, (C, C), 0)
    si = lax.broadcasted_iota(jnp.int32, (C, C), 1)
    tri = jnp.where((ti <= si) if rev else (ti >= si), 1.0, 0.0).astype(BF16)
    neg_inf = jnp.float32(-jnp.inf)
    base = 3 if rev else 0

    def stage_gates(rows, hd):
        cols = slice(hd * dh, (hd + 1) * dh)
        log_lb = tab_ref[base:base + 1, cols]
        log_1mlb = tab_ref[base + 1:base + 2, cols]
        one_mlb = tab_ref[base + 2:base + 3, cols]
        qr = q_ref[rows, cols]
        q = qr * _sigmoid(qr)
        v = i_ref[rows, cols]
        z = z_ref[rows, cols]
        e = jnp.exp(-jnp.abs(z))
        r = 1.0 / (1.0 + e)
        log_sig = jnp.minimum(z, 0.0) + jnp.log(r)
        k = one_mlb * (jnp.where(z >= 0.0, e, 1.0) * r)
        cc = log_1mlb + log_sig
        logf = jnp.maximum(log_lb, cc) + jnp.log1p(jnp.exp(-jnp.abs(log_lb - cc)))
        hi, mid, lo = _split3(logf)
        pieces = jnp.dot(tri, jnp.concatenate([hi, mid, lo], axis=1), preferred_element_type=F32)
        b = (pieces[:, :dh] + pieces[:, dh:2 * dh]) + pieces[:, 2 * dh:]
        return dict(q=q, k=k, v=v, b=b, cols=cols)

    def stage_state(hd, c):
        q, k, v, b = c["q"], c["k"], c["v"], c["b"]
        b_end = b[0:1, :] if rev else b[C - 1:C, :]
        st = st_ref[hd]
        qt = (q * jnp.exp(b)).astype(BF16)
        o = lax.dot_general(qt, st.astype(BF16), (((1,), (1,)), ((), ())), preferred_element_type=F32)
        kt = (k * jnp.exp(b_end - b)).astype(BF16)
        upd = lax.dot_general(v.astype(BF16), kt, (((0,), (0,)), ((), ())), preferred_element_type=F32)
        st_ref[hd] = st * jnp.exp(b_end) + upd
        c["halves"] = [o[0:SUBLANES, :], o[SUBLANES:C, :]]
        c["qh"] = [q[0:SUBLANES, :], q[SUBLANES:C, :]]
        c["bh"] = [b[0:SUBLANES, :], b[SUBLANES:C, :]]

    def stage_pair(s, c):
        b, k, v = c["b"], c["k"], c["v"]
        bs, ks, vs = b[s:s + 1, :], k[s:s + 1, :], v[s:s + 1, :]
        for h in range(2):
            lo_row, hi_row = h * SUBLANES, h * SUBLANES + SUBLANES - 1
            if rev:
                if lo_row > s:
                    continue
                full = hi_row <= s
            else:
                if hi_row < s:
                    continue
                full = lo_row >= s
            d = c["bh"][h] - bs
            if not full:
                keep = (row8 + lo_row <= s) if rev else (row8 + lo_row >= s)
                d = jnp.where(keep, d, neg_inf)
            p = (c["qh"][h] * ks) * jnp.exp(d)
            c["halves"][h] = c["halves"][h] + jnp.sum(p, axis=-1, keepdims=True) * vs

    def stage_out(rows, c):
        cols = c["cols"]
        o = jnp.concatenate(c["halves"], axis=0)
        if rev:
            tot = of_ref[rows, cols] + o
            nrm = tot * lax.rsqrt(jnp.mean(tot * tot, axis=-1, keepdims=True) + LN_EPS) * tab_ref[6:7, cols]
            gr = g_ref[rows, cols]
            o_ref[rows, cols] = (nrm * (gr * _sigmoid(gr))).astype(o_ref.dtype)
        else:
            o_ref[rows, cols] = o

    def body(ci, carry):
        c = (n_chunks - 1 - ci) if rev else ci
        rows = pl.ds(pl.multiple_of(c * C, C), C)
        ctx = [stage_gates(rows, hd) for hd in range(heads)]
        for hd in range(heads):
            stage_state(hd, ctx[hd])
        for s in range(C):
            for hd in range(heads):
                stage_pair(s, ctx[hd])
        for hd in range(heads):
            stage_out(rows, ctx[hd])
        return carry

    lax.fori_loop(0, n_chunks, body, 0)

    if emit_state:
        @pl.when(t == n_t - 1)
        def _():
            for hd in range(heads):
                sn_ref[hd] = st_ref[hd].T


def _hgrn_dir(proj, part, tab, state, layer_j, row0, n_seq, seq, rev, emit_state):
    has_state = state is not None
    G = HGRN_HEADS
    ts = min(HGRN_ROWS, seq)
    n_t = seq // ts
    hg_n = A_HEADS // G
    blk0 = row0 // ts
    tt = (lambda t: n_t - 1 - t) if rev else (lambda t: t)
    col = lambda k: (lambda b, hg, t: (blk0 + b * n_t + tt(t), k * hg_n + hg))
    in_specs = [pl.BlockSpec((ts, G * A_HEAD_DIM), col(k)) for k in (0, 1, 3 if rev else 2)]
    args = [proj, proj, proj]
    if rev:
        in_specs += [pl.BlockSpec((ts, G * A_HEAD_DIM), col(4)),
                     pl.BlockSpec((ts, G * A_HEAD_DIM), lambda b, hg, t: (b * n_t + tt(t), hg))]
        args += [proj, part]
    in_specs.append(pl.BlockSpec((SUBLANES, G * A_HEAD_DIM), lambda b, hg, t: (0, hg)))
    args.append(tab)
    if has_state:
        in_specs.append(pl.BlockSpec((None, None, None, G, A_HEAD_DIM, A_HEAD_DIM),
                                     lambda b, hg, t: (b, layer_j, 1 if rev else 0, hg, 0, 0)))
        args.append(state)
    out_specs = [pl.BlockSpec((ts, G * A_HEAD_DIM), lambda b, hg, t: (b * n_t + tt(t), hg))]
    out_shape = [jax.ShapeDtypeStruct((n_seq * seq, A_WIDTH), BF16 if rev else F32)]
    if emit_state:
        out_specs.append(pl.BlockSpec((None, G, A_HEAD_DIM, A_HEAD_DIM), lambda b, hg, t: (b, hg, 0, 0)))
        out_shape.append(jax.ShapeDtypeStruct((n_seq, A_HEADS, A_HEAD_DIM, A_HEAD_DIM), F32))
    return pl.pallas_call(
        functools.partial(_hgrn_kernel, rev=rev, ts=ts, heads=G, has_state=has_state, emit_state=emit_state),
        grid=(n_seq, hg_n, n_t),
        in_specs=in_specs,
        out_specs=out_specs,
        out_shape=out_shape,
        scratch_shapes=[pltpu.VMEM((G, A_HEAD_DIM, A_HEAD_DIM), F32)],
        compiler_params=_cparams(3),
        name="hgrn_bwd" if rev else "hgrn_fwd",
    )(*args)


def _hgrn(proj, tab, state, layer_j, row0, n_seq, seq, emit_state):
    fwd = _hgrn_dir(proj, None, tab, state, layer_j, row0, n_seq, seq, False, emit_state)
    bwd = _hgrn_dir(proj, fwd[0], tab, state, layer_j, row0, n_seq, seq, True, emit_state)
    if emit_state:
        return bwd[0], jnp.stack([fwd[1], bwd[1]], axis=1)
    return bwd[0], None


def _gelu_tanh(x):
    return 0.5 * x * (1.0 + jnp.tanh(math.sqrt(2.0 / math.pi) * (x + 0.044715 * (x * x * x))))


def _sgu_kernel(u_ref, v_ref, lng_ref, lnb_ref, w_ref, bias_ref, o_ref):
    v = _gelu_tanh(v_ref[...])
    mu = jnp.mean(v, axis=-1, keepdims=True)
    vc = v - mu
    var = jnp.mean(vc * vc, axis=-1, keepdims=True)
    vn = (vc * lax.rsqrt(var + LN_EPS) * lng_ref[...] + lnb_ref[...]).astype(BF16)
    for g in range(B_GROUPS):
        sl = slice(g * SGU_CHUNK, (g + 1) * SGU_CHUNK)
        mixed = jnp.dot(w_ref[g], vn[:, sl], preferred_element_type=F32) + bias_ref[:, sl]
        o_ref[:, sl] = (_gelu_tanh(u_ref[:, sl]) * mixed).astype(o_ref.dtype)


def _sgu(proj, ln_g, ln_b, w_s, bias_full):
    rows = proj.shape[0]
    ublk = 5 * A_WIDTH // B_WIDTH
    return pl.pallas_call(
        _sgu_kernel,
        grid=(rows // SGU_CHUNK,),
        in_specs=[pl.BlockSpec((SGU_CHUNK, B_WIDTH), lambda r: (r, ublk)),
                  pl.BlockSpec((SGU_CHUNK, B_WIDTH), lambda r: (r, ublk + 1)),
                  pl.BlockSpec((1, B_WIDTH), lambda r: (0, 0)),
                  pl.BlockSpec((1, B_WIDTH), lambda r: (0, 0)),
                  pl.BlockSpec((B_GROUPS, SGU_CHUNK, SGU_CHUNK), lambda r: (0, 0, 0)),
                  pl.BlockSpec((SGU_CHUNK, B_WIDTH), lambda r: (0, 0))],
        out_specs=pl.BlockSpec((SGU_CHUNK, B_WIDTH), lambda r: (r, 0)),
        out_shape=jax.ShapeDtypeStruct((rows, B_WIDTH), BF16),
        compiler_params=_cparams(1),
        name="sgu",
    )(proj, proj, ln_g, ln_b, w_s, bias_full)


def _attn_kernel(*refs, with_cache, kb):
    if with_cache:
        (q_ref, cos_ref, sin_ref, kv_ref, kpe_ref, kvc_ref, kpec_ref, o_ref,
         kx_ref, vx_ref, s_ref, kxc_ref, vxc_ref, sc_ref) = refs
    else:
        q_ref, cos_ref, sin_ref, kv_ref, kpe_ref, o_ref, kx_ref, vx_ref, s_ref = refs
    nt = (((1,), (1,)), ((), ()))
    pair = NOPE_DIM + V_DIM
    qw = NOPE_DIM + 2 * ROPE_DIM
    tq = q_ref.shape[0]
    seq = kv_ref.shape[0]

    @pl.when(pl.program_id(2) == 0)
    def _():
        pad = qw - NOPE_DIM - ROPE_DIM
        for i in range(2):
            kx_ref[i, :, 0:NOPE_DIM] = kv_ref[:, i * pair:i * pair + NOPE_DIM]
            kx_ref[i, :, NOPE_DIM:NOPE_DIM + ROPE_DIM] = kpe_ref[...]
            kx_ref[i, :, NOPE_DIM + ROPE_DIM:qw] = jnp.zeros((seq, pad), BF16)
            vx_ref[i, :, 0:V_DIM] = kv_ref[:, i * pair + NOPE_DIM:(i + 1) * pair]
            vx_ref[i, :, V_DIM:2 * V_DIM] = jnp.ones((seq, V_DIM), BF16)
            if with_cache:
                past = kvc_ref.shape[0]
                kxc_ref[i, :, 0:NOPE_DIM] = kvc_ref[:, i * pair:i * pair + NOPE_DIM]
                kxc_ref[i, :, NOPE_DIM:NOPE_DIM + ROPE_DIM] = kpec_ref[...]
                kxc_ref[i, :, NOPE_DIM + ROPE_DIM:qw] = jnp.zeros((past, pad), BF16)
                vxc_ref[i, :, 0:V_DIM] = kvc_ref[:, i * pair + NOPE_DIM:(i + 1) * pair]
                vxc_ref[i, :, V_DIM:2 * V_DIM] = jnp.ones((past, V_DIM), BF16)

    cos, sin = cos_ref[...], sin_ref[...]
    n_kb = seq // kb
    qs = []
    for i in range(2):
        q0 = i * qw
        qpe = (q_ref[:, q0 + NOPE_DIM:q0 + NOPE_DIM + ROPE_DIM].astype(F32) * cos
               + q_ref[:, q0 + NOPE_DIM + ROPE_DIM:q0 + qw].astype(F32) * sin).astype(BF16)
        qs.append(jnp.concatenate([q_ref[:, q0:q0 + NOPE_DIM], qpe,
                                   jnp.zeros((tq, qw - NOPE_DIM - ROPE_DIM), BF16)], axis=1))
    m = [None, None]
    for j in range(n_kb):
        for i in range(2):
            s = lax.dot_general(qs[i], kx_ref[i, j * kb:(j + 1) * kb, :], nt, preferred_element_type=F32)
            s_ref[i, :, j * kb:(j + 1) * kb] = s
            bm = jnp.max(s, axis=-1, keepdims=True)
            m[i] = bm if m[i] is None else jnp.maximum(m[i], bm)
    if with_cache:
        for i in range(2):
            s = lax.dot_general(qs[i], kxc_ref[i], nt, preferred_element_type=F32)
            sc_ref[i] = s
            m[i] = jnp.maximum(m[i], jnp.max(s, axis=-1, keepdims=True))
    acc = [jnp.zeros((tq, 2 * V_DIM), F32), jnp.zeros((tq, 2 * V_DIM), F32)]
    for j in range(n_kb):
        for i in range(2):
            p = jnp.exp2(s_ref[i, :, j * kb:(j + 1) * kb] - m[i]).astype(BF16)
            acc[i] = acc[i] + jnp.dot(p, vx_ref[i, j * kb:(j + 1) * kb, :], preferred_element_type=F32)
    for i in range(2):
        a = acc[i]
        if with_cache:
            a = a + jnp.dot(jnp.exp2(sc_ref[i] - m[i]).astype(BF16), vxc_ref[i], preferred_element_type=F32)
        o_ref[:, i * V_DIM:(i + 1) * V_DIM] = (a[:, 0:V_DIM] / a[:, V_DIM:V_DIM + 1]).astype(o_ref.dtype)


def _attn(q, cos_t, sin_t, kv, kpe, row0, n_seq, seq, tq, cache=None):
    nq = seq // tq
    qb0, kb0 = row0 // tq, row0 // seq
    pair = 2 * (NOPE_DIM + V_DIM)
    qw = NOPE_DIM + 2 * ROPE_DIM
    kb = min(ATTN_KB, seq)
    qrow = lambda b, h, i: (qb0 + b * nq + i, 0)
    in_specs = [pl.BlockSpec((tq, 2 * qw), lambda b, h, i: (qb0 + b * nq + i, h)),
                pl.BlockSpec((tq, ROPE_DIM), qrow),
                pl.BlockSpec((tq, ROPE_DIM), qrow),
                pl.BlockSpec((seq, pair), lambda b, h, i: (kb0 + b, h)),
                pl.BlockSpec((seq, ROPE_DIM), lambda b, h, i: (kb0 + b, 0))]
    args = [q, cos_t, sin_t, kv, kpe]
    scratch = [pltpu.VMEM((2, seq, qw), BF16), pltpu.VMEM((2, seq, 2 * V_DIM), BF16), pltpu.VMEM((2, tq, seq), F32)]
    if cache is not None:
        kvc, kpec, past = cache
        in_specs += [pl.BlockSpec((past, pair), lambda b, h, i: (b, h)),
                     pl.BlockSpec((past, ROPE_DIM), lambda b, h, i: (b, 0))]
        args += [kvc, kpec]
        scratch += [pltpu.VMEM((2, past, qw), BF16), pltpu.VMEM((2, past, 2 * V_DIM), BF16),
                    pltpu.VMEM((2, tq, past), F32)]
    return pl.pallas_call(
        functools.partial(_attn_kernel, with_cache=cache is not None, kb=kb),
        grid=(n_seq, C_HEADS // 2, nq),
        in_specs=in_specs,
        out_specs=pl.BlockSpec((tq, 2 * V_DIM), lambda b, h, i: (b * nq + i, h)),
        out_shape=jax.ShapeDtypeStruct((n_seq * seq, C_HEADS * V_DIM), BF16),
        scratch_shapes=scratch,
        compiler_params=_cparams(3),
        name="mla_attn",
    )(*args)


def _moe_up_kernel(be_ref, nu_ref, x_ref, wg_ref, wu_ref, h_ref, wgb_ref, wub_ref):
    i = pl.program_id(1)
    fresh = jnp.logical_or(i == 0, be_ref[i] != be_ref[jnp.maximum(i - 1, 0)])

    @pl.when(fresh)
    def _():
        wgb_ref[...] = wg_ref[...].astype(BF16)
        wub_ref[...] = wu_ref[...].astype(BF16)

    @pl.when(i < nu_ref[0])
    def _():
        x = x_ref[...]
        a = jnp.dot(x, wgb_ref[...], preferred_element_type=F32)
        b = jnp.dot(x, wub_ref[...], preferred_element_type=F32)
        h_ref[...] = (a * _sigmoid(a) * b).astype(h_ref.dtype)

    @pl.when(i >= nu_ref[0])
    def _():
        h_ref[...] = jnp.zeros_like(h_ref)


def _moe_down_kernel(be_ref, nu_ref, h_ref, wd_ref, y_ref, wdb_ref):
    i = pl.program_id(1)
    fresh = jnp.logical_or(i == 0, be_ref[i] != be_ref[jnp.maximum(i - 1, 0)])

    @pl.when(fresh)
    def _():
        wdb_ref[...] = wd_ref[...].astype(BF16)

    @pl.when(i < nu_ref[0])
    def _():
        y_ref[...] = jnp.dot(h_ref[...], wdb_ref[...], preferred_element_type=F32)

    @pl.when(i >= nu_ref[0])
    def _():
        y_ref[...] = jnp.zeros_like(y_ref)


def _moe_ffn(xp, blk_expert, n_used, w_gate, w_up, w_down, layer):
    n_rows, D = xp.shape
    FF = w_gate.shape[-1]
    n_blk = n_rows // MOE_TM
    h = pl.pallas_call(
        _moe_up_kernel,
        grid_spec=pltpu.PrefetchScalarGridSpec(
            num_scalar_prefetch=2,
            grid=(FF // MOE_TF, n_blk),
            in_specs=[pl.BlockSpec((MOE_TM, D), lambda f, i, be, nu: (i, 0)),
                      pl.BlockSpec((None, None, D, MOE_TF), lambda f, i, be, nu: (layer, be[i], 0, f)),
                      pl.BlockSpec((None, None, D, MOE_TF), lambda f, i, be, nu: (layer, be[i], 0, f))],
            out_specs=pl.BlockSpec((MOE_TM, MOE_TF), lambda f, i, be, nu: (i, f)),
            scratch_shapes=[pltpu.VMEM((D, MOE_TF), BF16), pltpu.VMEM((D, MOE_TF), BF16)]),
        out_shape=jax.ShapeDtypeStruct((n_rows, FF), BF16),
        compiler_params=_cparams(2),
        name="moe_up",
    )(blk_expert, n_used, xp, w_gate, w_up)
    return pl.pallas_call(
        _moe_down_kernel,
        grid_spec=pltpu.PrefetchScalarGridSpec(
            num_scalar_prefetch=2,
            grid=(D // MOE_TN, n_blk),
            in_specs=[pl.BlockSpec((MOE_TM, FF), lambda n, i, be, nu: (i, 0)),
                      pl.BlockSpec((None, None, FF, MOE_TN), lambda n, i, be, nu: (layer, be[i], 0, n))],
            out_specs=pl.BlockSpec((MOE_TM, MOE_TN), lambda n, i, be, nu: (i, n)),
            scratch_shapes=[pltpu.VMEM((FF, MOE_TN), BF16)]),
        out_shape=jax.ShapeDtypeStruct((n_rows, D), F32),
        compiler_params=_cparams(2),
        name="moe_down",
    )(blk_expert, n_used, h, w_down)


def _route(logits_t, b_router_t):
    n_tok = logits_t.shape[1]
    scores = jax.nn.sigmoid(logits_t).reshape(N_GROUPS, SUBLANES, n_tok)
    r_io = lax.broadcasted_iota(jnp.int32, (N_GROUPS, SUBLANES, n_tok), 1)
    g_io = lax.broadcasted_iota(jnp.int32, (N_GROUPS, n_tok), 0)
    sel = jnp.where(r_io < EXPERTS_PER_GROUP, scores + b_router_t.reshape(N_GROUPS, SUBLANES, 1), -jnp.inf)
    m1 = jnp.max(sel, axis=1)
    i1 = jnp.min(jnp.where(sel == m1[:, None, :], r_io, SUBLANES), axis=1)
    sel2 = jnp.where(r_io == i1[:, None, :], -jnp.inf, sel)
    m2 = jnp.max(sel2, axis=1)
    i2 = jnp.min(jnp.where(sel2 == m2[:, None, :], r_io, SUBLANES), axis=1)
    group_score = m1 + m2
    g_idx = jnp.min(jnp.where(group_score == jnp.max(group_score, axis=0, keepdims=True), g_io, N_GROUPS), axis=0)
    best = g_io == g_idx[None, :]
    l1 = jnp.sum(jnp.where(best, i1, 0), axis=0)
    l2 = jnp.sum(jnp.where(best, i2, 0), axis=0)
    local = jnp.stack([l1, l2], axis=0)
    pick = best[None, :, None, :] & (r_io[None] == local[:, None, None, :])
    s = jnp.sum(jnp.where(pick, scores[None], 0.0), axis=(1, 2))
    gate = s / jnp.sum(s, axis=0, keepdims=True)
    return (g_idx[None, :] * EXPERTS_PER_GROUP + local).astype(jnp.int32), gate


def _dispatch(expert_idx):
    n_tok = expert_idx.shape[1]
    n_flat = TOP_K * n_tok
    nb = n_flat // LANES
    e_flat = expert_idx.reshape(nb, LANES)
    onehot = e_flat[None] == lax.broadcasted_iota(jnp.int32, (N_EXPERTS, nb, LANES), 0)
    upper = (lax.broadcasted_iota(jnp.int32, (LANES, LANES), 0)
             <= lax.broadcasted_iota(jnp.int32, (LANES, LANES), 1)).astype(BF16)
    within = jnp.einsum('ebl,lm->ebm', onehot.astype(BF16), upper, preferred_element_type=F32)
    blk_cnt = within[:, :, LANES - 1]
    blk_end = jnp.cumsum(blk_cnt, axis=1)
    counts = blk_end[:, -1].astype(jnp.int32)
    padded = (counts + MOE_TM - 1) // MOE_TM * MOE_TM
    pad_end = jnp.cumsum(padded)
    pad_start = pad_end - padded
    offs = (blk_end - blk_cnt)[:, :, None] + pad_start.astype(F32)[:, None, None] + within - 1.0
    pos = jnp.sum(jnp.where(onehot, offs, 0.0), axis=0).astype(jnp.int32).reshape(n_flat)
    n_blk = (n_flat + N_EXPERTS * (MOE_TM - 1) + MOE_TM - 1) // MOE_TM
    blk_expert = jnp.minimum(jnp.searchsorted(pad_end, jnp.arange(n_blk, dtype=jnp.int32) * MOE_TM, side='right'),
                             N_EXPERTS - 1).astype(jnp.int32)
    n_used = (pad_end[-1:] // MOE_TM).astype(jnp.int32)
    tok = lax.rem(jnp.arange(n_flat, dtype=jnp.int32), n_tok)
    src = lax.rem(jnp.arange(n_blk * MOE_TM, dtype=jnp.int32), n_tok).at[pos].set(tok, unique_indices=True)
    return pos.reshape(TOP_K, n_tok), blk_expert, n_used, src


def _moe(x, logits_t, b_router_t, w_gate, w_up, w_down, layer):
    expert_idx, gate = _route(logits_t, b_router_t)
    pos, blk_expert, n_used, src = _dispatch(expert_idx)
    xp = jnp.take(x, src, axis=0)
    yp = _moe_ffn(xp, blk_expert, n_used, w_gate, w_up, w_down, layer)
    return yp[pos[0]] * gate[0][:, None] + yp[pos[1]] * gate[1][:, None]


def _rmsnorm(x, g):
    return x * lax.rsqrt(jnp.mean(x * x, axis=-1, keepdims=True) + LN_EPS) * g


def _rot_cols(w):
    shp = w.shape
    w4 = w.reshape(shp[:-1] + (shp[-1] // 32, 2, 16))
    return jnp.stack([-w4[..., 1, :], w4[..., 0, :]], axis=-2).reshape(shp)


def _rope_tables(n_prompt_rows, n_seq, seq):
    t = jnp.arange(seq)
    row = (t // GRID_W).astype(F32)
    colp = (t % GRID_W).astype(F32)
    n_freq = ROPE_DIM // 4
    inv = ROPE_THETA ** (-jnp.arange(n_freq, dtype=F32) / n_freq)
    ang = jnp.stack([row[:, None] * inv, colp[:, None] * inv], axis=1)
    cos = jnp.broadcast_to(jnp.cos(ang)[:, :, None, :], (seq, 2, 2, n_freq)).reshape(seq, ROPE_DIM)
    sin = jnp.broadcast_to(jnp.sin(ang)[:, :, None, :], (seq, 2, 2, n_freq)).reshape(seq, ROPE_DIM)
    cos = jnp.concatenate([jnp.ones((n_prompt_rows, ROPE_DIM), F32), jnp.tile(cos, (n_seq, 1))], axis=0)
    sin = jnp.concatenate([jnp.zeros((n_prompt_rows, ROPE_DIM), F32), jnp.tile(sin, (n_seq, 1))], axis=0)
    return cos, sin


def _router_weights(w_router, b_router):
    D = w_router.shape[0]
    wt = w_router.T.reshape(N_GROUPS, EXPERTS_PER_GROUP, D)
    wt = jnp.pad(wt, ((0, 0), (0, SUBLANES - EXPERTS_PER_GROUP), (0, 0))).reshape(ROUTER_ROWS, D)
    bt = jnp.pad(b_router.astype(F32).reshape(N_GROUPS, EXPERTS_PER_GROUP),
                 ((0, 0), (0, SUBLANES - EXPERTS_PER_GROUP))).reshape(ROUTER_ROWS)
    return _split2(wt.astype(F32)), bt


def kernel(x_prompt, x_sample, state_hgrn, cache_ckv, cache_kpe, c, c_ctx, w_ada, b_ada, ln_g, ln_b, w_in_ab, hgrn_lb, hgrn_norm_g, sgu_ln_g, sgu_ln_b, sgu_w, sgu_b, w_out_ab, w_in_c, q_norm_g, kv_norm_g, w_uq, w_ukv, w_o_c, w_router, b_router, w_gate_e, w_up_e, w_down_e):
    n_p, s_p, D = x_prompt.shape
    n_s, s_s, _ = x_sample.shape
    past = cache_ckv.shape[2]
    rows_p, rows_s = n_p * s_p, n_s * s_s
    rows = rows_p + rows_s
    assert rows_p % COND_ROWS == 0 and s_s == COND_ROWS
    n_grp = rows // COND_ROWS
    grp_p = rows_p // COND_ROWS

    h = jnp.concatenate([x_prompt.reshape(rows_p, D), x_sample.reshape(rows_s, D)], axis=0)

    cond = jnp.concatenate([jnp.broadcast_to(c_ctx[None, :], (grp_p, D)), c], axis=0)
    cond_rows = -(-n_grp // 16) * 16
    cond = jnp.pad(jax.nn.silu(cond.astype(F32)), ((0, cond_rows - n_grp), (0, 0))).astype(BF16)
    b_ada3 = b_ada.reshape(DEPTH, 1, ADA_MULT * D)

    def modulation(layer):
        mod = _ada_mm(cond, w_ada, b_ada3, layer)[:n_grp]
        return [m[:, None, :] for m in jnp.split(mod, ADA_MULT, axis=-1)]

    p = jax.nn.softmax(hgrn_lb.astype(F32), axis=0)
    lbs = jnp.cumsum(p, axis=0) - p[0:1]

    cos_t, sin_t = _rope_tables(rows_p, n_s, s_s)
    w_router_t, b_router_t = _router_weights(w_router, b_router)
    q_scale = math.log2(math.e) / math.sqrt(NOPE_DIM + ROPE_DIM)

    hg_states, ckv_list, kpe_list = [], [], []
    sh1, sc1, g1, sh2, sc2, g2 = modulation(0)
    (a,) = _ln_call("first", h, sc=sc1, sh=sh1)
    for layer in range(DEPTH):
        j = layer // 2
        if layer % 2 == 0:
            proj = _mm(a, w_in_ab[j].astype(BF16), F32)
            lb = lbs[j]
            tab = jnp.concatenate([jnp.log(lb[0:1]), jnp.log1p(-lb[0:1]), 1.0 - lb[0:1],
                                   jnp.log(lb[1:2]), jnp.log1p(-lb[1:2]), 1.0 - lb[1:2],
                                   hgrn_norm_g[j][None, :], jnp.zeros((1, A_WIDTH), F32)], axis=0)
            oa_p, st_p = _hgrn(proj, tab, None, j, 0, n_p, s_p, True)
            oa_s, _ = _hgrn(proj, tab, state_hgrn, j, rows_p, n_s, s_s, False)
            hg_states.append(st_p)
            bias_full = jnp.repeat(sgu_b[j].T, SGU_CHUNK, axis=1)
            o_sgu = _sgu(proj, sgu_ln_g[j][None, :], sgu_ln_b[j][None, :], sgu_w[j].astype(BF16), bias_full)
            m = _mm2(jnp.concatenate([oa_p, oa_s], axis=0), o_sgu, w_out_ab[j].astype(BF16), F32)
        else:
            w_kpe = w_in_c[j][:, Q_LORA + KV_LORA:]
            n_c = Q_LORA + KV_LORA + 2 * ROPE_DIM
            w_in_ext = jnp.concatenate([w_in_c[j], _rot_cols(w_kpe), jnp.zeros((D, 2048 - n_c), F32)], axis=1)
            pc = _mm(a, w_in_ext.astype(BF16), F32)
            cq = _rmsnorm(pc[:, :Q_LORA], q_norm_g[j]).astype(BF16)
            ckv = _rmsnorm(pc[:, Q_LORA:Q_LORA + KV_LORA], kv_norm_g[j])
            kpe_raw = pc[:, Q_LORA + KV_LORA:Q_LORA + KV_LORA + ROPE_DIM]
            kpe_rot = pc[:, Q_LORA + KV_LORA + ROPE_DIM:n_c]
            kpe = (kpe_raw * cos_t + kpe_rot * sin_t).astype(BF16)
            ckv_list.append(ckv[:rows_p].reshape(n_p, s_p, KV_LORA))
            kpe_list.append(kpe_raw[:rows_p].reshape(n_p, s_p, ROPE_DIM))
            wq = w_uq[j].reshape(Q_LORA, C_HEADS, NOPE_DIM + ROPE_DIM) * q_scale
            wq_p = wq[:, :, NOPE_DIM:]
            wq_ext = jnp.concatenate([wq[:, :, :NOPE_DIM], wq_p, _rot_cols(wq_p)], axis=2)
            q = _mm(cq, wq_ext.reshape(Q_LORA, C_HEADS * (NOPE_DIM + 2 * ROPE_DIM)).astype(BF16), BF16)
            w_ukv_b = w_ukv[j].astype(BF16)
            kv = _mm(ckv.astype(BF16), w_ukv_b, BF16)
            kvc = _mm(cache_ckv[:, j].reshape(n_s * past, KV_LORA).astype(BF16), w_ukv_b, BF16)
            kpec = cache_kpe[:, j].reshape(n_s * past, ROPE_DIM).astype(BF16)
            o_p = _attn(q, cos_t, sin_t, kv, kpe, 0, n_p, s_p, s_p)
            o_s = _attn(q, cos_t, sin_t, kv, kpe, rows_p, n_s, s_s, ATTN_TQ, cache=(kvc, kpec, past))
            m = _mm(jnp.concatenate([o_p, o_s], axis=0), w_o_c[j].astype(BF16), F32)
        h, bmod, logits_t = _ln_call("moe", h, m, g1, ln_g[layer, 0][None, :], ln_b[layer, 0][None, :],
                                     sc2, sh2, w_router_t)
        y = _moe(bmod, logits_t, b_router_t, w_gate_e, w_up_e, w_down_e, layer)
        if layer + 1 < DEPTH:
            sh1, sc1, g1, sh2n, sc2n, g2n = modulation(layer + 1)
            h, a = _ln_call("mixer", h, y, g2, ln_g[layer, 1][None, :], ln_b[layer, 1][None, :], sc1, sh1)
            sh2, sc2, g2 = sh2n, sc2n, g2n
        else:
            (h,) = _ln_call("last", h, y, g2, ln_g[layer, 1][None, :], ln_b[layer, 1][None, :])

    y_prompt = h[:rows_p].reshape(n_p, s_p, D)
    y_sample = h[rows_p:].reshape(n_s, s_s, D)
    return (y_prompt, y_sample, jnp.stack(hg_states, axis=1), jnp.stack(ckv_list, axis=1), jnp.stack(kpe_list, axis=1))
```

```python
import functools
import math

import jax
import jax.numpy as jnp
from jax import lax
from jax.experimental import pallas as pl
from jax.experimental.pallas import tpu as pltpu

F32 = jnp.float32
BF16 = jnp.bfloat16

D_MODEL = 4096
DEPTH = 4
GRID_W = 64
A_WIDTH = D_MODEL // 2
A_HEAD_DIM = 128
A_HEADS = A_WIDTH // A_HEAD_DIM
B_WIDTH = D_MODEL // 2
SGU_CHUNK = 128
B_GROUPS = 16
C_HEADS = D_MODEL // 128
Q_LORA = D_MODEL // 4
KV_LORA = 512
NOPE_DIM = 128
ROPE_DIM = 64
V_DIM = 128
ROPE_THETA = 10000.0
N_EXPERTS = 48
N_GROUPS = 8
EXPERTS_PER_GROUP = N_EXPERTS // N_GROUPS
TOP_K = 2
EXPERT_FF = D_MODEL // 4
ADA_MULT = 6
DEEPNORM_ALPHA = (2.0 * DEPTH) ** 0.25
LN_EPS = 1e-6

LANES = 128
SUBLANES = 8
COND_ROWS = 4096
HGRN_BLOCK = 16
HGRN_HEADS = 16
HGRN_ROWS = 256
LN_ROWS = 256
ROUTER_ROWS = N_GROUPS * SUBLANES
ATTN_TQ = 512
ATTN_KB = 512
MOE_TM = 512
MOE_TF = 512
MOE_TN = 2048
VMEM_LIMIT = 56 * 1024 * 1024


def _cparams(n_axes):
    return pltpu.CompilerParams(dimension_semantics=("arbitrary",) * n_axes, vmem_limit_bytes=VMEM_LIMIT)


def _split2(a):
    hi = a.astype(BF16)
    return hi, (a - hi.astype(F32)).astype(BF16)


def _split3(a):
    hi = a.astype(BF16)
    r1 = a - hi.astype(F32)
    mid = r1.astype(BF16)
    lo = (r1 - mid.astype(F32)).astype(BF16)
    return hi, mid, lo


def _sigmoid(x):
    return 1.0 / (1.0 + jnp.exp(-x))


def _mm_kernel(x_ref, w_ref, o_ref):
    o_ref[...] = jnp.dot(x_ref[...], w_ref[...], preferred_element_type=F32).astype(o_ref.dtype)


def _mm(x, w, out_dtype, tm=1024, tn=1024):
    M, K = x.shape
    N = w.shape[1]
    tm, tn = min(tm, M), min(tn, N)
    assert M % tm == 0 and N % tn == 0
    return pl.pallas_call(
        _mm_kernel,
        grid=(N // tn, M // tm),
        in_specs=[pl.BlockSpec((tm, K), lambda n, m: (m, 0)),
                  pl.BlockSpec((K, tn), lambda n, m: (0, n))],
        out_specs=pl.BlockSpec((tm, tn), lambda n, m: (m, n)),
        out_shape=jax.ShapeDtypeStruct((M, N), out_dtype),
        compiler_params=_cparams(2),
        name="dense_mm",
    )(x, w)


def _mm2_kernel(x1_ref, x2_ref, w1_ref, w2_ref, o_ref):
    o_ref[...] = (jnp.dot(x1_ref[...], w1_ref[...], preferred_element_type=F32)
                  + jnp.dot(x2_ref[...], w2_ref[...], preferred_element_type=F32)).astype(o_ref.dtype)


def _mm2(x1, x2, w, out_dtype, tm=1024, tn=1024):
    M, K1 = x1.shape
    K2 = x2.shape[1]
    N = w.shape[1]
    assert K1 == K2 and w.shape[0] == K1 + K2
    return pl.pallas_call(
        _mm2_kernel,
        grid=(N // tn, M // tm),
        in_specs=[pl.BlockSpec((tm, K1), lambda n, m: (m, 0)),
                  pl.BlockSpec((tm, K2), lambda n, m: (m, 0)),
                  pl.BlockSpec((K1, tn), lambda n, m: (0, n)),
                  pl.BlockSpec((K2, tn), lambda n, m: (1, n))],
        out_specs=pl.BlockSpec((tm, tn), lambda n, m: (m, n)),
        out_shape=jax.ShapeDtypeStruct((M, N), out_dtype),
        compiler_params=_cparams(2),
        name="dense_mm2",
    )(x1, x2, w, w)


def _ada_kernel(x_ref, w_ref, b_ref, o_ref):
    o_ref[...] = jnp.dot(x_ref[...], w_ref[...].astype(BF16), preferred_element_type=F32) + b_ref[...]


def _ada_mm(x, w_all, b_all, layer, tn=1024):
    R, K = x.shape
    N = w_all.shape[2]
    return pl.pallas_call(
        _ada_kernel,
        grid=(N // tn,),
        in_specs=[pl.BlockSpec((R, K), lambda n: (0, 0)),
                  pl.BlockSpec((None, K, tn), lambda n: (layer, 0, n)),
                  pl.BlockSpec((None, 1, tn), lambda n: (layer, 0, n))],
        out_specs=pl.BlockSpec((R, tn), lambda n: (0, n)),
        out_shape=jax.ShapeDtypeStruct((R, N), F32),
        compiler_params=_cparams(1),
        name="ada_mm",
    )(x, w_all, b_all)


def _ln_kernel(*refs, mode):
    if mode == "first":
        h_ref, sc_ref, sh_ref, mod_ref = refs
        hn = h_ref[...]
    else:
        h_ref, m_ref, gate_ref, lng_ref, lnb_ref = refs[:5]
        x = DEEPNORM_ALPHA * h_ref[...] + gate_ref[...] * m_ref[...]
        mu = jnp.mean(x, axis=-1, keepdims=True)
        xc = x - mu
        var = jnp.mean(xc * xc, axis=-1, keepdims=True)
        hn = xc * lax.rsqrt(var + LN_EPS) * lng_ref[...] + lnb_ref[...]
        if mode == "last":
            refs[5][...] = hn
            return
        sc_ref, sh_ref = refs[5:7]
        if mode == "moe":
            wh_ref, wm_ref, hn_ref, mod_ref, lt_ref = refs[7:]
        else:
            hn_ref, mod_ref = refs[7:]
        hn_ref[...] = hn
    mod = hn * (1.0 + sc_ref[...]) + sh_ref[...]
    mod_ref[...] = mod.astype(mod_ref.dtype)
    if mode == "moe":
        xh, xm = _split2(mod)
        nt = (((1,), (1,)), ((), ()))
        dot = functools.partial(lax.dot_general, dimension_numbers=nt, preferred_element_type=F32)
        lt_ref[...] = dot(wh_ref[...], xh) + (dot(wh_ref[...], xm) + dot(wm_ref[...], xh))


def _ln_call(mode, h, m=None, gate=None, ln_g=None, ln_b=None, sc=None, sh=None, w_router=None):
    rows, D = h.shape
    tm = LN_ROWS
    grp = lambda i: (i * tm // COND_ROWS, 0, 0)
    row_spec = pl.BlockSpec((tm, D), lambda i: (i, 0))
    cond_spec = pl.BlockSpec((None, 1, D), grp)
    vec_spec = pl.BlockSpec((1, D), lambda i: (0, 0))
    args, in_specs = [h], [row_spec]
    if mode != "first":
        args += [m, gate, ln_g, ln_b]
        in_specs += [row_spec, cond_spec, vec_spec, vec_spec]
    if mode != "last":
        args += [sc, sh]
        in_specs += [cond_spec, cond_spec]
    out_shape, out_specs = [], []
    if mode != "first":
        out_shape.append(jax.ShapeDtypeStruct((rows, D), F32))
        out_specs.append(row_spec)
    if mode != "last":
        out_shape.append(jax.ShapeDtypeStruct((rows, D), BF16))
        out_specs.append(row_spec)
    if mode == "moe":
        args += list(w_router)
        in_specs += [pl.BlockSpec((ROUTER_ROWS, D), lambda i: (0, 0))] * 2
        out_shape.append(jax.ShapeDtypeStruct((ROUTER_ROWS, rows), F32))
        out_specs.append(pl.BlockSpec((ROUTER_ROWS, tm), lambda i: (0, i)))
    return pl.pallas_call(
        functools.partial(_ln_kernel, mode=mode),
        grid=(rows // tm,),
        in_specs=in_specs,
        out_specs=out_specs,
        out_shape=out_shape,
        compiler_params=_cparams(1),
        name="ln_" + mode,
    )(*args)


def _hgrn_kernel(*refs, rev, ts, heads, has_state, emit_state):
    refs = list(refs)
    q_ref, i_ref, z_ref = refs[:3]
    pos = 3
    if rev:
        g_ref, of_ref = refs[pos:pos + 2]
        pos += 2
    tab_ref = refs[pos]
    pos += 1
    s0_ref = None
    if has_state:
        s0_ref = refs[pos]
        pos += 1
    o_ref = refs[pos]
    pos += 1
    sn_ref = None
    if emit_state:
        sn_ref = refs[pos]
        pos += 1
    st_ref = refs[pos]

    t = pl.program_id(2)
    n_t = pl.num_programs(2)
    C = HGRN_BLOCK
    n_chunks = ts // C
    dh = A_HEAD_DIM

    @pl.when(t == 0)
    def _():
        for hd in range(heads):
            if has_state:
                st_ref[hd] = s0_ref[hd].T
            else:
                st_ref[hd] = jnp.zeros((dh, dh), F32)

    row8 = lax.broadcasted_iota(jnp.int32, (SUBLANES, dh), 0)
    ti = lax.broadcasted_iota(jnp.int32, (C, C), 0)
    si = lax.broadcasted_iota(jnp.int32, (C, C), 1)
    tri = jnp.where((ti <= si) if rev else (ti >= si), 1.0, 0.0).astype(BF16)
    neg_inf = jnp.float32(-jnp.inf)
    base = 3 if rev else 0

    def stage_gates(rows, hd):
        cols = slice(hd * dh, (hd + 1) * dh)
        log_lb = tab_ref[base:base + 1, cols]
        log_1mlb = tab_ref[base + 1:base + 2, cols]
        one_mlb = tab_ref[base + 2:base + 3, cols]
        qr = q_ref[rows, cols]
        q = qr * _sigmoid(qr)
        v = i_ref[rows, cols]
        z = z_ref[rows, cols]
        e = jnp.exp(-jnp.abs(z))
        r = 1.0 / (1.0 + e)
        log_sig = jnp.minimum(z, 0.0) + jnp.log(r)
        k = one_mlb * (jnp.where(z >= 0.0, e, 1.0) * r)
        cc = log_1mlb + log_sig
        logf = jnp.maximum(log_lb, cc) + jnp.log1p(jnp.exp(-jnp.abs(log_lb - cc)))
        hi, mid, lo = _split3(logf)
        pieces = jnp.dot(tri, jnp.concatenate([hi, mid, lo], axis=1), preferred_element_type=F32)
        b = (pieces[:, :dh] + pieces[:, dh:2 * dh]) + pieces[:, 2 * dh:]
        return dict(q=q, k=k, v=v, b=b, cols=cols)

    def stage_state(hd, c):
        q, k, v, b = c["q"], c["k"], c["v"], c["b"]
        b_end = b[0:1, :] if rev else b[C - 1:C, :]
        st = st_ref[hd]
        qt = (q * jnp.exp(b)).astype(BF16)
        o = lax.dot_general(qt, st.astype(BF16), (((1,), (1,)), ((), ())), preferred_element_type=F32)
        kt = (k * jnp.exp(b_end - b)).astype(BF16)
        upd = lax.dot_general(v.astype(BF16), kt, (((0,), (0,)), ((), ())), preferred_element_type=F32)
        st_ref[hd] = st * jnp.exp(b_end) + upd
        c["halves"] = [o[0:SUBLANES, :], o[SUBLANES:C, :]]
        c["qh"] = [q[0:SUBLANES, :], q[SUBLANES:C, :]]
        c["bh"] = [b[0:SUBLANES, :], b[SUBLANES:C, :]]

    def stage_pair(s, c):
        b, k, v = c["b"], c["k"], c["v"]
        bs, ks, vs = b[s:s + 1, :], k[s:s + 1, :], v[s:s + 1, :]
        for h in range(2):
            lo_row, hi_row = h * SUBLANES, h * SUBLANES + SUBLANES - 1
            if rev:
                if lo_row > s:
                    continue
                full = hi_row <= s
            else:
                if hi_row < s:
                    continue
                full = lo_row >= s
            d = c["bh"][h] - bs
            if not full:
                keep = (row8 + lo_row <= s) if rev else (row8 + lo_row >= s)
                d = jnp.where(keep, d, neg_inf)
            p = (c["qh"][h] * ks) * jnp.exp(d)
            c["halves"][h] = c["halves"][h] + jnp.sum(p, axis=-1, keepdims=True) * vs

    def stage_out(rows, c):
        cols = c["cols"]
        o = jnp.concatenate(c["halves"], axis=0)
        if rev:
            tot = of_ref[rows, cols] + o
            nrm = tot * lax.rsqrt(jnp.mean(tot * tot, axis=-1, keepdims=True) + LN_EPS) * tab_ref[6:7, cols]
            gr = g_ref[rows, cols]
            o_ref[rows, cols] = (nrm * (gr * _sigmoid(gr))).astype(o_ref.dtype)
        else:
            o_ref[rows, cols] = o

    def body(ci, carry):
        c = (n_chunks - 1 - ci) if rev else ci
        rows = pl.ds(pl.multiple_of(c * C, C), C)
        ctx = [stage_gates(rows, hd) for hd in range(heads)]
        for hd in range(heads):
            stage_state(hd, ctx[hd])
        for s in range(C):
            for hd in range(heads):
                stage_pair(s, ctx[hd])
        for hd in range(heads):
            stage_out(rows, ctx[hd])
        return carry

    lax.fori_loop(0, n_chunks, body, 0)

    if emit_state:
        @pl.when(t == n_t - 1)
        def _():
            for hd in range(heads):
                sn_ref[hd] = st_ref[hd].T


def _hgrn_dir(proj, part, tab, state, layer_j, row0, n_seq, seq, rev, emit_state):
    has_state = state is not None
    G = HGRN_HEADS
    ts = min(HGRN_ROWS, seq)
    n_t = seq // ts
    hg_n = A_HEADS // G
    blk0 = row0 // ts
    tt = (lambda t: n_t - 1 - t) if rev else (lambda t: t)
    col = lambda k: (lambda b, hg, t: (blk0 + b * n_t + tt(t), k * hg_n + hg))
    in_specs = [pl.BlockSpec((ts, G * A_HEAD_DIM), col(k)) for k in (0, 1, 3 if rev else 2)]
    args = [proj, proj, proj]
    if rev:
        in_specs += [pl.BlockSpec((ts, G * A_HEAD_DIM), col(4)),
                     pl.BlockSpec((ts, G * A_HEAD_DIM), lambda b, hg, t: (b * n_t + tt(t), hg))]
        args += [proj, part]
    in_specs.append(pl.BlockSpec((SUBLANES, G * A_HEAD_DIM), lambda b, hg, t: (0, hg)))
    args.append(tab)
    if has_state:
        in_specs.append(pl.BlockSpec((None, None, None, G, A_HEAD_DIM, A_HEAD_DIM),
                                     lambda b, hg, t: (b, layer_j, 1 if rev else 0, hg, 0, 0)))
        args.append(state)
    out_specs = [pl.BlockSpec((ts, G * A_HEAD_DIM), lambda b, hg, t: (b * n_t + tt(t), hg))]
    out_shape = [jax.ShapeDtypeStruct((n_seq * seq, A_WIDTH), BF16 if rev else F32)]
    if emit_state:
        out_specs.append(pl.BlockSpec((None, G, A_HEAD_DIM, A_HEAD_DIM), lambda b, hg, t: (b, hg, 0, 0)))
        out_shape.append(jax.ShapeDtypeStruct((n_seq, A_HEADS, A_HEAD_DIM, A_HEAD_DIM), F32))
    return pl.pallas_call(
        functools.partial(_hgrn_kernel, rev=rev, ts=ts, heads=G, has_state=has_state, emit_state=emit_state),
        grid=(n_seq, hg_n, n_t),
        in_specs=in_specs,
        out_specs=out_specs,
        out_shape=out_shape,
        scratch_shapes=[pltpu.VMEM((G, A_HEAD_DIM, A_HEAD_DIM), F32)],
        compiler_params=_cparams(3),
        name="hgrn_bwd" if rev else "hgrn_fwd",
    )(*args)


def _hgrn(proj, tab, state, layer_j, row0, n_seq, seq, emit_state):
    fwd = _hgrn_dir(proj, None, tab, state, layer_j, row0, n_seq, seq, False, emit_state)
    bwd = _hgrn_dir(proj, fwd[0], tab, state, layer_j, row0, n_seq, seq, True, emit_state)
    if emit_state:
        return bwd[0], jnp.stack([fwd[1], bwd[1]], axis=1)
    return bwd[0], None


def _gelu_tanh(x):
    return 0.5 * x * (1.0 + jnp.tanh(math.sqrt(2.0 / math.pi) * (x + 0.044715 * (x * x * x))))


def _sgu_kernel(u_ref, v_ref, lng_ref, lnb_ref, w_ref, bias_ref, o_ref):
    v = _gelu_tanh(v_ref[...])
    mu = jnp.mean(v, axis=-1, keepdims=True)
    vc = v - mu
    var = jnp.mean(vc * vc, axis=-1, keepdims=True)
    vn = (vc * lax.rsqrt(var + LN_EPS) * lng_ref[...] + lnb_ref[...]).astype(BF16)
    for g in range(B_GROUPS):
        sl = slice(g * SGU_CHUNK, (g + 1) * SGU_CHUNK)
        mixed = jnp.dot(w_ref[g], vn[:, sl], preferred_element_type=F32) + bias_ref[:, sl]
        o_ref[:, sl] = (_gelu_tanh(u_ref[:, sl]) * mixed).astype(o_ref.dtype)


def _sgu(proj, ln_g, ln_b, w_s, bias_full):
    rows = proj.shape[0]
    ublk = 5 * A_WIDTH // B_WIDTH
    return pl.pallas_call(
        _sgu_kernel,
        grid=(rows // SGU_CHUNK,),
        in_specs=[pl.BlockSpec((SGU_CHUNK, B_WIDTH), lambda r: (r, ublk)),
                  pl.BlockSpec((SGU_CHUNK, B_WIDTH), lambda r: (r, ublk + 1)),
                  pl.BlockSpec((1, B_WIDTH), lambda r: (0, 0)),
                  pl.BlockSpec((1, B_WIDTH), lambda r: (0, 0)),
                  pl.BlockSpec((B_GROUPS, SGU_CHUNK, SGU_CHUNK), lambda r: (0, 0, 0)),
                  pl.BlockSpec((SGU_CHUNK, B_WIDTH), lambda r: (0, 0))],
        out_specs=pl.BlockSpec((SGU_CHUNK, B_WIDTH), lambda r: (r, 0)),
        out_shape=jax.ShapeDtypeStruct((rows, B_WIDTH), BF16),
        compiler_params=_cparams(1),
        name="sgu",
    )(proj, proj, ln_g, ln_b, w_s, bias_full)


def _attn_kernel(*refs, with_cache, kb):
    if with_cache:
        (q_ref, cos_ref, sin_ref, kv_ref, kpe_ref, kvc_ref, kpec_ref, o_ref,
         kx_ref, vx_ref, s_ref, kxc_ref, vxc_ref, sc_ref) = refs
    else:
        q_ref, cos_ref, sin_ref, kv_ref, kpe_ref, o_ref, kx_ref, vx_ref, s_ref = refs
    nt = (((1,), (1,)), ((), ()))
    pair = NOPE_DIM + V_DIM
    qw = NOPE_DIM + 2 * ROPE_DIM
    tq = q_ref.shape[0]
    seq = kv_ref.shape[0]

    @pl.when(pl.program_id(2) == 0)
    def _():
        pad = qw - NOPE_DIM - ROPE_DIM
        for i in range(2):
            kx_ref[i, :, 0:NOPE_DIM] = kv_ref[:, i * pair:i * pair + NOPE_DIM]
            kx_ref[i, :, NOPE_DIM:NOPE_DIM + ROPE_DIM] = kpe_ref[...]
            kx_ref[i, :, NOPE_DIM + ROPE_DIM:qw] = jnp.zeros((seq, pad), BF16)
            vx_ref[i, :, 0:V_DIM] = kv_ref[:, i * pair + NOPE_DIM:(i + 1) * pair]
            vx_ref[i, :, V_DIM:2 * V_DIM] = jnp.ones((seq, V_DIM), BF16)
            if with_cache:
                past = kvc_ref.shape[0]
                kxc_ref[i, :, 0:NOPE_DIM] = kvc_ref[:, i * pair:i * pair + NOPE_DIM]
                kxc_ref[i, :, NOPE_DIM:NOPE_DIM + ROPE_DIM] = kpec_ref[...]
                kxc_ref[i, :, NOPE_DIM + ROPE_DIM:qw] = jnp.zeros((past, pad), BF16)
                vxc_ref[i, :, 0:V_DIM] = kvc_ref[:, i * pair + NOPE_DIM:(i + 1) * pair]
                vxc_ref[i, :, V_DIM:2 * V_DIM] = jnp.ones((past, V_DIM), BF16)

    cos, sin = cos_ref[...], sin_ref[...]
    n_kb = seq // kb
    qs = []
    for i in range(2):
        q0 = i * qw
        qpe = (q_ref[:, q0 + NOPE_DIM:q0 + NOPE_DIM + ROPE_DIM].astype(F32) * cos
               + q_ref[:, q0 + NOPE_DIM + ROPE_DIM:q0 + qw].astype(F32) * sin).astype(BF16)
        qs.append(jnp.concatenate([q_ref[:, q0:q0 + NOPE_DIM], qpe,
                                   jnp.zeros((tq, qw - NOPE_DIM - ROPE_DIM), BF16)], axis=1))
    m = [None, None]
    for j in range(n_kb):
        for i in range(2):
            s = lax.dot_general(qs[i], kx_ref[i, j * kb:(j + 1) * kb, :], nt, preferred_element_type=F32)
            s_ref[i, :, j * kb:(j + 1) * kb] = s
            bm = jnp.max(s, axis=-1, keepdims=True)
            m[i] = bm if m[i] is None else jnp.maximum(m[i], bm)
    if with_cache:
        for i in range(2):
            s = lax.dot_general(qs[i], kxc_ref[i], nt, preferred_element_type=F32)
            sc_ref[i] = s
            m[i] = jnp.maximum(m[i], jnp.max(s, axis=-1, keepdims=True))
    acc = [jnp.zeros((tq, 2 * V_DIM), F32), jnp.zeros((tq, 2 * V_DIM), F32)]
    for j in range(n_kb):
        for i in range(2):
            p = jnp.exp2(s_ref[i, :, j * kb:(j + 1) * kb] - m[i]).astype(BF16)
            acc[i] = acc[i] + jnp.dot(p, vx_ref[i, j * kb:(j + 1) * kb, :], preferred_element_type=F32)
    for i in range(2):
        a = acc[i]
        if with_cache:
            a = a + jnp.dot(jnp.exp2(sc_ref[i] - m[i]).astype(BF16), vxc_ref[i], preferred_element_type=F32)
        o_ref[:, i * V_DIM:(i + 1) * V_DIM] = (a[:, 0:V_DIM] / a[:, V_DIM:V_DIM + 1]).astype(o_ref.dtype)


def _attn(q, cos_t, sin_t, kv, kpe, row0, n_seq, seq, tq, cache=None):
    nq = seq // tq
    qb0, kb0 = row0 // tq, row0 // seq
    pair = 2 * (NOPE_DIM + V_DIM)
    qw = NOPE_DIM + 2 * ROPE_DIM
    kb = min(ATTN_KB, seq)
    qrow = lambda b, h, i: (qb0 + b * nq + i, 0)
    in_specs = [pl.BlockSpec((tq, 2 * qw), lambda b, h, i: (qb0 + b * nq + i, h)),
                pl.BlockSpec((tq, ROPE_DIM), qrow),
                pl.BlockSpec((tq, ROPE_DIM), qrow),
                pl.BlockSpec((seq, pair), lambda b, h, i: (kb0 + b, h)),
                pl.BlockSpec((seq, ROPE_DIM), lambda b, h, i: (kb0 + b, 0))]
    args = [q, cos_t, sin_t, kv, kpe]
    scratch = [pltpu.VMEM((2, seq, qw), BF16), pltpu.VMEM((2, seq, 2 * V_DIM), BF16), pltpu.VMEM((2, tq, seq), F32)]
    if cache is not None:
        kvc, kpec, past = cache
        in_specs += [pl.BlockSpec((past, pair), lambda b, h, i: (b, h)),
                     pl.BlockSpec((past, ROPE_DIM), lambda b, h, i: (b, 0))]
        args += [kvc, kpec]
        scratch += [pltpu.VMEM((2, past, qw), BF16), pltpu.VMEM((2, past, 2 * V_DIM), BF16),
                    pltpu.VMEM((2, tq, past), F32)]
    return pl.pallas_call(
        functools.partial(_attn_kernel, with_cache=cache is not None, kb=kb),
        grid=(n_seq, C_HEADS // 2, nq),
        in_specs=in_specs,
        out_specs=pl.BlockSpec((tq, 2 * V_DIM), lambda b, h, i: (b * nq + i, h)),
        out_shape=jax.ShapeDtypeStruct((n_seq * seq, C_HEADS * V_DIM), BF16),
        scratch_shapes=scratch,
        compiler_params=_cparams(3),
        name="mla_attn",
    )(*args)


def _moe_up_kernel(be_ref, nu_ref, x_ref, wg_ref, wu_ref, h_ref, wgb_ref, wub_ref):
    i = pl.program_id(1)
    fresh = jnp.logical_or(i == 0, be_ref[i] != be_ref[jnp.maximum(i - 1, 0)])

    @pl.when(fresh)
    def _():
        wgb_ref[...] = wg_ref[...].astype(BF16)
        wub_ref[...] = wu_ref[...].astype(BF16)

    @pl.when(i < nu_ref[0])
    def _():
        x = x_ref[...]
        a = jnp.dot(x, wgb_ref[...], preferred_element_type=F32)
        b = jnp.dot(x, wub_ref[...], preferred_element_type=F32)
        h_ref[...] = (a * _sigmoid(a) * b).astype(h_ref.dtype)

    @pl.when(i >= nu_ref[0])
    def _():
        h_ref[...] = jnp.zeros_like(h_ref)


def _moe_down_kernel(be_ref, nu_ref, h_ref, wd_ref, y_ref, wdb_ref):
    i = pl.program_id(1)
    fresh = jnp.logical_or(i == 0, be_ref[i] != be_ref[jnp.maximum(i - 1, 0)])

    @pl.when(fresh)
    def _():
        wdb_ref[...] = wd_ref[...].astype(BF16)

    @pl.when(i < nu_ref[0])
    def _():
        y_ref[...] = jnp.dot(h_ref[...], wdb_ref[...], preferred_element_type=F32)

    @pl.when(i >= nu_ref[0])
    def _():
        y_ref[...] = jnp.zeros_like(y_ref)


def _moe_ffn(xp, blk_expert, n_used, w_gate, w_up, w_down, layer):
    n_rows, D = xp.shape
    FF = w_gate.shape[-1]
    n_blk = n_rows // MOE_TM
    h = pl.pallas_call(
        _moe_up_kernel,
        grid_spec=pltpu.PrefetchScalarGridSpec(
            num_scalar_prefetch=2,
            grid=(FF // MOE_TF, n_blk),
            in_specs=[pl.BlockSpec((MOE_TM, D), lambda f, i, be, nu: (i, 0)),
                      pl.BlockSpec((None, None, D, MOE_TF), lambda f, i, be, nu: (layer, be[i], 0, f)),
                      pl.BlockSpec((None, None, D, MOE_TF), lambda f, i, be, nu: (layer, be[i], 0, f))],
            out_specs=pl.BlockSpec((MOE_TM, MOE_TF), lambda f, i, be, nu: (i, f)),
            scratch_shapes=[pltpu.VMEM((D, MOE_TF), BF16), pltpu.VMEM((D, MOE_TF), BF16)]),
        out_shape=jax.ShapeDtypeStruct((n_rows, FF), BF16),
        compiler_params=_cparams(2),
        name="moe_up",
    )(blk_expert, n_used, xp, w_gate, w_up)
    return pl.pallas_call(
        _moe_down_kernel,
        grid_spec=pltpu.PrefetchScalarGridSpec(
            num_scalar_prefetch=2,
            grid=(D // MOE_TN, n_blk),
            in_specs=[pl.BlockSpec((MOE_TM, FF), lambda n, i, be, nu: (i, 0)),
                      pl.BlockSpec((None, None, FF, MOE_TN), lambda n, i, be, nu: (layer, be[i], 0, n))],
            out_specs=pl.BlockSpec((MOE_TM, MOE_TN), lambda n, i, be, nu: (i, n)),
            scratch_shapes=[pltpu.VMEM((FF, MOE_TN), BF16)]),
        out_shape=jax.ShapeDtypeStruct((n_rows, D), F32),
        compiler_params=_cparams(2),
        name="moe_down",
    )(blk_expert, n_used, h, w_down)


def _route(logits_t, b_router_t):
    n_tok = logits_t.shape[1]
    scores = jax.nn.sigmoid(logits_t).reshape(N_GROUPS, SUBLANES, n_tok)
    r_io = lax.broadcasted_iota(jnp.int32, (N_GROUPS, SUBLANES, n_tok), 1)
    g_io = lax.broadcasted_iota(jnp.int32, (N_GROUPS, n_tok), 0)
    sel = jnp.where(r_io < EXPERTS_PER_GROUP, scores + b_router_t.reshape(N_GROUPS, SUBLANES, 1), -jnp.inf)
    m1 = jnp.max(sel, axis=1)
    i1 = jnp.min(jnp.where(sel == m1[:, None, :], r_io, SUBLANES), axis=1)
    sel2 = jnp.where(r_io == i1[:, None, :], -jnp.inf, sel)
    m2 = jnp.max(sel2, axis=1)
    i2 = jnp.min(jnp.where(sel2 == m2[:, None, :], r_io, SUBLANES), axis=1)
    group_score = m1 + m2
    g_idx = jnp.min(jnp.where(group_score == jnp.max(group_score, axis=0, keepdims=True), g_io, N_GROUPS), axis=0)
    best = g_io == g_idx[None, :]
    l1 = jnp.sum(jnp.where(best, i1, 0), axis=0)
    l2 = jnp.sum(jnp.where(best, i2, 0), axis=0)
    local = jnp.stack([l1, l2], axis=0)
    pick = best[None, :, None, :] & (r_io[None] == local[:, None, None, :])
    s = jnp.sum(jnp.where(pick, scores[None], 0.0), axis=(1, 2))
    gate = s / jnp.sum(s, axis=0, keepdims=True)
    return (g_idx[None, :] * EXPERTS_PER_GROUP + local).astype(jnp.int32), gate


def _dispatch(expert_idx):
    n_tok = expert_idx.shape[1]
    n_flat = TOP_K * n_tok
    nb = n_flat // LANES
    e_flat = expert_idx.reshape(nb, LANES)
    onehot = e_flat[None] == lax.broadcasted_iota(jnp.int32, (N_EXPERTS, nb, LANES), 0)
    upper = (lax.broadcasted_iota(jnp.int32, (LANES, LANES), 0)
             <= lax.broadcasted_iota(jnp.int32, (LANES, LANES), 1)).astype(BF16)
    within = jnp.einsum('ebl,lm->ebm', onehot.astype(BF16), upper, preferred_element_type=F32)
    blk_cnt = within[:, :, LANES - 1]
    blk_end = jnp.cumsum(blk_cnt, axis=1)
    counts = blk_end[:, -1].astype(jnp.int32)
    padded = (counts + MOE_TM - 1) // MOE_TM * MOE_TM
    pad_end = jnp.cumsum(padded)
    pad_start = pad_end - padded
    offs = (blk_end - blk_cnt)[:, :, None] + pad_start.astype(F32)[:, None, None] + within - 1.0
    pos = jnp.sum(jnp.where(onehot, offs, 0.0), axis=0).astype(jnp.int32).reshape(n_flat)
    n_blk = (n_flat + N_EXPERTS * (MOE_TM - 1) + MOE_TM - 1) // MOE_TM
    blk_expert = jnp.minimum(jnp.searchsorted(pad_end, jnp.arange(n_blk, dtype=jnp.int32) * MOE_TM, side='right'),
                             N_EXPERTS - 1).astype(jnp.int32)
    n_used = (pad_end[-1:] // MOE_TM).astype(jnp.int32)
    tok = lax.rem(jnp.arange(n_flat, dtype=jnp.int32), n_tok)
    src = lax.rem(jnp.arange(n_blk * MOE_TM, dtype=jnp.int32), n_tok).at[pos].set(tok, unique_indices=True)
    return pos.reshape(TOP_K, n_tok), blk_expert, n_used, src


def _moe(x, logits_t, b_router_t, w_gate, w_up, w_down, layer):
    expert_idx, gate = _route(logits_t, b_router_t)
    pos, blk_expert, n_used, src = _dispatch(expert_idx)
    xp = jnp.take(x, src, axis=0)
    yp = _moe_ffn(xp, blk_expert, n_used, w_gate, w_up, w_down, layer)
    return yp[pos[0]] * gate[0][:, None] + yp[pos[1]] * gate[1][:, None]


def _rmsnorm(x, g):
    return x * lax.rsqrt(jnp.mean(x * x, axis=-1, keepdims=True) + LN_EPS) * g


def _rot_cols(w):
    shp = w.shape
    w4 = w.reshape(shp[:-1] + (shp[-1] // 32, 2, 16))
    return jnp.stack([-w4[..., 1, :], w4[..., 0, :]], axis=-2).reshape(shp)


def _rope_tables(n_prompt_rows, n_seq, seq):
    t = jnp.arange(seq)
    row = (t // GRID_W).astype(F32)
    colp = (t % GRID_W).astype(F32)
    n_freq = ROPE_DIM // 4
    inv = ROPE_THETA ** (-jnp.arange(n_freq, dtype=F32) / n_freq)
    ang = jnp.stack([row[:, None] * inv, colp[:, None] * inv], axis=1)
    cos = jnp.broadcast_to(jnp.cos(ang)[:, :, None, :], (seq, 2, 2, n_freq)).reshape(seq, ROPE_DIM)
    sin = jnp.broadcast_to(jnp.sin(ang)[:, :, None, :], (seq, 2, 2, n_freq)).reshape(seq, ROPE_DIM)
    cos = jnp.concatenate([jnp.ones((n_prompt_rows, ROPE_DIM), F32), jnp.tile(cos, (n_seq, 1))], axis=0)
    sin = jnp.concatenate([jnp.zeros((n_prompt_rows, ROPE_DIM), F32), jnp.tile(sin, (n_seq, 1))], axis=0)
    return cos, sin


def _router_weights(w_router, b_router):
    D = w_router.shape[0]
    wt = w_router.T.reshape(N_GROUPS, EXPERTS_PER_GROUP, D)
    wt = jnp.pad(wt, ((0, 0), (0, SUBLANES - EXPERTS_PER_GROUP), (0, 0))).reshape(ROUTER_ROWS, D)
    bt = jnp.pad(b_router.astype(F32).reshape(N_GROUPS, EXPERTS_PER_GROUP),
                 ((0, 0), (0, SUBLANES - EXPERTS_PER_GROUP))).reshape(ROUTER_ROWS)
    return _split2(wt.astype(F32)), bt


def kernel(x_prompt, x_sample, state_hgrn, cache_ckv, cache_kpe, c, c_ctx, w_ada, b_ada, ln_g, ln_b, w_in_ab, hgrn_lb, hgrn_norm_g, sgu_ln_g, sgu_ln_b, sgu_w, sgu_b, w_out_ab, w_in_c, q_norm_g, kv_norm_g, w_uq, w_ukv, w_o_c, w_router, b_router, w_gate_e, w_up_e, w_down_e):
    n_p, s_p, D = x_prompt.shape
    n_s, s_s, _ = x_sample.shape
    past = cache_ckv.shape[2]
    rows_p, rows_s = n_p * s_p, n_s * s_s
    rows = rows_p + rows_s
    assert rows_p % COND_ROWS == 0 and s_s == COND_ROWS
    n_grp = rows // COND_ROWS
    grp_p = rows_p // COND_ROWS

    h = jnp.concatenate([x_prompt.reshape(rows_p, D), x_sample.reshape(rows_s, D)], axis=0)

    cond = jnp.concatenate([jnp.broadcast_to(c_ctx[None, :], (grp_p, D)), c], axis=0)
    cond_rows = -(-n_grp // 16) * 16
    cond = jnp.pad(jax.nn.silu(cond.astype(F32)), ((0, cond_rows - n_grp), (0, 0))).astype(BF16)
    b_ada3 = b_ada.reshape(DEPTH, 1, ADA_MULT * D)

    def modulation(layer):
        mod = _ada_mm(cond, w_ada, b_ada3, layer)[:n_grp]
        return [m[:, None, :] for m in jnp.split(mod, ADA_MULT, axis=-1)]

    p = jax.nn.softmax(hgrn_lb.astype(F32), axis=0)
    lbs = jnp.cumsum(p, axis=0) - p[0:1]

    cos_t, sin_t = _rope_tables(rows_p, n_s, s_s)
    w_router_t, b_router_t = _router_weights(w_router, b_router)
    q_scale = math.log2(math.e) / math.sqrt(NOPE_DIM + ROPE_DIM)

    hg_states, ckv_list, kpe_list = [], [], []
    sh1, sc1, g1, sh2, sc2, g2 = modulation(0)
    (a,) = _ln_call("first", h, sc=sc1, sh=sh1)
    for layer in range(DEPTH):
        j = layer // 2
        if layer % 2 == 0:
            proj = _mm(a, w_in_ab[j].astype(BF16), F32)
            lb = lbs[j]
            tab = jnp.concatenate([jnp.log(lb[0:1]), jnp.log1p(-lb[0:1]), 1.0 - lb[0:1],
                                   jnp.log(lb[1:2]), jnp.log1p(-lb[1:2]), 1.0 - lb[1:2],
                                   hgrn_norm_g[j][None, :], jnp.zeros((1, A_WIDTH), F32)], axis=0)
            oa_p, st_p = _hgrn(proj, tab, None, j, 0, n_p, s_p, True)
            oa_s, _ = _hgrn(proj, tab, state_hgrn, j, rows_p, n_s, s_s, False)
            hg_states.append(st_p)
            bias_full = jnp.repeat(sgu_b[j].T, SGU_CHUNK, axis=1)
            o_sgu = _sgu(proj, sgu_ln_g[j][None, :], sgu_ln_b[j][None, :], sgu_w[j].astype(BF16), bias_full)
            m = _mm2(jnp.concatenate([oa_p, oa_s], axis=0), o_sgu, w_out_ab[j].astype(BF16), F32)
        else:
            w_kpe = w_in_c[j][:, Q_LORA + KV_LORA:]
            n_c = Q_LORA + KV_LORA + 2 * ROPE_DIM
            w_in_ext = jnp.concatenate([w_in_c[j], _rot_cols(w_kpe), jnp.zeros((D, 2048 - n_c), F32)], axis=1)
            pc = _mm(a, w_in_ext.astype(BF16), F32)
            cq = _rmsnorm(pc[:, :Q_LORA], q_norm_g[j]).astype(BF16)
            ckv = _rmsnorm(pc[:, Q_LORA:Q_LORA + KV_LORA], kv_norm_g[j])
            kpe_raw = pc[:, Q_LORA + KV_LORA:Q_LORA + KV_LORA + ROPE_DIM]
            kpe_rot = pc[:, Q_LORA + KV_LORA + ROPE_DIM:n_c]
            kpe = (kpe_raw * cos_t + kpe_rot * sin_t).astype(BF16)
            ckv_list.append(ckv[:rows_p].reshape(n_p, s_p, KV_LORA))
            kpe_list.append(kpe_raw[:rows_p].reshape(n_p, s_p, ROPE_DIM))
            wq = w_uq[j].reshape(Q_LORA, C_HEADS, NOPE_DIM + ROPE_DIM) * q_scale
            wq_p = wq[:, :, NOPE_DIM:]
            wq_ext = jnp.concatenate([wq[:, :, :NOPE_DIM], wq_p, _rot_cols(wq_p)], axis=2)
            q = _mm(cq, wq_ext.reshape(Q_LORA, C_HEADS * (NOPE_DIM + 2 * ROPE_DIM)).astype(BF16), BF16)
            w_ukv_b = w_ukv[j].astype(BF16)
            kv = _mm(ckv.astype(BF16), w_ukv_b, BF16)
            kvc = _mm(cache_ckv[:, j].reshape(n_s * past, KV_LORA).astype(BF16), w_ukv_b, BF16)
            kpec = cache_kpe[:, j].reshape(n_s * past, ROPE_DIM).astype(BF16)
            o_p = _attn(q, cos_t, sin_t, kv, kpe, 0, n_p, s_p, s_p)
            o_s = _attn(q, cos_t, sin_t, kv, kpe, rows_p, n_s, s_s, ATTN_TQ, cache=(kvc, kpec, past))
            m = _mm(jnp.concatenate([o_p, o_s], axis=0), w_o_c[j].astype(BF16), F32)
        h, bmod, logits_t = _ln_call("moe", h, m, g1, ln_g[layer, 0][None, :], ln_b[layer, 0][None, :],
                                     sc2, sh2, w_router_t)
        y = _moe(bmod, logits_t, b_router_t, w_gate_e, w_up_e, w_down_e, layer)
        if layer + 1 < DEPTH:
            sh1, sc1, g1, sh2n, sc2n, g2n = modulation(layer + 1)
            h, a = _ln_call("mixer", h, y, g2, ln_g[layer, 1][None, :], ln_b[layer, 1][None, :], sc1, sh1)
            sh2, sc2, g2 = sh2n, sc2n, g2n
        else:
            (h,) = _ln_call("last", h, y, g2, ln_g[layer, 1][None, :], ln_b[layer, 1][None, :])

    y_prompt = h[:rows_p].reshape(n_p, s_p, D)
    y_sample = h[rows_p:].reshape(n_s, s_s, D)
    return (y_prompt, y_sample, jnp.stack(hg_states, axis=1), jnp.stack(ckv_list, axis=1), jnp.stack(kpe_list, axis=1))
```
